```python
import jax, jax.numpy as jnp
from jax import lax
import numpy as np

D_MODEL = 2048
BATCH = 8
SEQ = 2048
DEPTH = 2

CHUNK = 64
Q_BLOCK = 2 * CHUNK
N_META = 16
N_HEADS = D_MODEL // 128
HEAD_DIM = 64
ATT_W = N_HEADS * HEAD_DIM
CONV_C = D_MODEL // 2
CONV_K = 31
N_GROUPS = 4
EXPERTS_PER_GROUP = 8
N_EXPERTS = N_GROUPS * EXPERTS_PER_GROUP
TOP_K = 2
D_EXPERT = (3 * D_MODEL) // 8
IN_COLS = 3 * ATT_W + N_HEADS + 2 * CONV_C + 2 * D_MODEL
RMS_EPS = 1e-6
LN_EPS = 1e-5

kernel_name = 'fox_conformer_hmoe_hybrid_block'


def rmsnorm(x, g):
    x32 = x.astype(jnp.float32)
    y = x32 * lax.rsqrt(jnp.mean(x32 * x32, axis=-1, keepdims=True) + RMS_EPS)
    return (y * g.astype(jnp.float32)).astype(x.dtype)


def forgetting_attention(q, k, v, log_f):
    B, L, H, Dh = q.shape
    Lp = -(-L // Q_BLOCK) * Q_BLOCK
    pad = Lp - L
    pad4 = ((0, 0), (0, pad), (0, 0), (0, 0))
    q = jnp.pad(q, pad4)
    k = jnp.pad(k, pad4)
    v = jnp.pad(v, pad4)
    cum = jnp.cumsum(jnp.pad(log_f, ((0, 0), (0, pad), (0, 0))), axis=1)
    cum = cum.transpose(0, 2, 1)
    scale = HEAD_DIM ** -0.5
    outs = []
    for i in range(Lp // Q_BLOCK):
        q0, q1 = i * Q_BLOCK, (i + 1) * Q_BLOCK
        s = jnp.einsum('bqhd,bkhd->bhqk', q[:, q0:q1], k[:, :q1]).astype(jnp.float32) * scale
        bias = cum[:, :, q0:q1, None] - cum[:, :, None, :q1]
        causal = (q0 + jnp.arange(Q_BLOCK))[:, None] >= jnp.arange(q1)[None, :]
        s = jnp.where(causal, s + bias, -jnp.inf)
        p = jax.nn.softmax(s, axis=-1)
        outs.append(jnp.einsum('bhqk,bkhd->bqhd', p.astype(v.dtype), v[:, :q1]))
    return jnp.concatenate(outs, axis=1)[:, :L]


def conformer_conv(u, conv_w, conv_b, ln_g, ln_b):
    a, b = jnp.split(u, 2, axis=-1)
    z = a * jax.nn.sigmoid(b)
    z = lax.conv_general_dilated(
        z, conv_w.astype(z.dtype)[:, None, :], window_strides=(1,),
        padding=[(CONV_K - 1, 0)], dimension_numbers=('NWC', 'WIO', 'NWC'),
        feature_group_count=CONV_C) + conv_b.astype(z.dtype)
    z32 = z.astype(jnp.float32)
    mu = jnp.mean(z32, axis=-1, keepdims=True)
    var = jnp.mean(jnp.square(z32 - mu), axis=-1, keepdims=True)
    zn = (z32 - mu) * lax.rsqrt(var + LN_EPS) * ln_g.astype(jnp.float32) + ln_b.astype(jnp.float32)
    return jax.nn.silu(zn).astype(u.dtype)


def hierarchical_moe(h, w_rg, b_rg, w_re, b_re, w_gu, w_dn):
    B, L, D = h.shape
    t = h.reshape(-1, D)
    g_logits = (t @ w_rg).astype(jnp.float32) + b_rg.astype(jnp.float32)
    g_prob = jax.nn.softmax(g_logits, axis=-1)
    g_idx = jnp.argmax(g_logits, axis=-1)
    g_w = jnp.take_along_axis(g_prob, g_idx[:, None], axis=-1)[:, 0]
    e_logits = ((t @ w_re).astype(jnp.float32) + b_re.astype(jnp.float32)).reshape(-1, N_GROUPS, EXPERTS_PER_GROUP)
    e_in_group = jnp.take_along_axis(e_logits, g_idx[:, None, None], axis=1)[:, 0]
    top_v, top_i = lax.top_k(e_in_group, TOP_K)
    w = g_w[:, None] * jax.nn.softmax(top_v, axis=-1)
    e_idx = g_idx[:, None] * EXPERTS_PER_GROUP + top_i
    combine = jnp.sum(jax.nn.one_hot(e_idx, N_EXPERTS, dtype=jnp.float32) * w[..., None], axis=1)
    combine = combine.astype(t.dtype)
    out = jnp.zeros_like(t)
    for e in range(N_EXPERTS):
        a, b = jnp.split(t @ w_gu[e], 2, axis=-1)
        out = out + combine[:, e:e + 1] * ((jax.nn.silu(a) * b) @ w_dn[e])
    return out.reshape(B, L, D)


def setup_inputs(seed: int = 0) -> dict:
    key = jax.random.key(seed)
    ks = jax.random.split(key, 20)
    f32 = jnp.float32
    nrm = lambda k, shape, fan_in: jax.random.normal(k, shape, f32) * (fan_in ** -0.5)
    gain = lambda k, shape: 1.0 + 0.02 * jax.random.normal(k, shape, f32)
    return {
        'x': jax.random.normal(ks[0], (BATCH, SEQ, D_MODEL), f32),
        'meta': jax.random.normal(ks[1], (N_META, D_MODEL), f32),
        'norm_mix': gain(ks[2], (DEPTH, D_MODEL)),
        'w_in': nrm(ks[3], (DEPTH, D_MODEL, IN_COLS), D_MODEL),
        'b_forget': jax.random.uniform(ks[4], (DEPTH, N_HEADS), f32, 1.0, 6.0),
        'w_attn_out': nrm(ks[5], (DEPTH, ATT_W, D_MODEL), ATT_W),
        'conv_w': nrm(ks[6], (DEPTH, CONV_K, CONV_C), CONV_K),
        'conv_b': 0.02 * jax.random.normal(ks[7], (DEPTH, CONV_C), f32),
        'conv_ln_g': gain(ks[8], (DEPTH, CONV_C)),
        'conv_ln_b': 0.02 * jax.random.normal(ks[9], (DEPTH, CONV_C), f32),
        'w_conv_out': nrm(ks[10], (DEPTH, CONV_C, D_MODEL), CONV_C),
        'w_out': nrm(ks[11], (DEPTH, D_MODEL, D_MODEL), D_MODEL),
        'norm_ffn': gain(ks[12], (DEPTH, D_MODEL)),
        'w_router_group': nrm(ks[13], (DEPTH, D_MODEL, N_GROUPS), D_MODEL),
        'b_router_group': 0.01 * jax.random.normal(ks[14], (DEPTH, N_GROUPS), f32),
        'w_router_expert': nrm(ks[15], (DEPTH, D_MODEL, N_EXPERTS), D_MODEL),
        'b_router_expert': 0.01 * jax.random.normal(ks[16], (DEPTH, N_EXPERTS), f32),
        'w_gate_up': nrm(ks[17], (DEPTH, N_EXPERTS, D_MODEL, 2 * D_EXPERT), D_MODEL),
        'w_down': nrm(ks[18], (DEPTH, N_EXPERTS, D_EXPERT, D_MODEL), D_EXPERT),
        'norm_final': gain(ks[19], (D_MODEL,)),
    }


def reference(x, meta, norm_mix, w_in, b_forget, w_attn_out, conv_w, conv_b, conv_ln_g, conv_ln_b,
              w_conv_out, w_out, norm_ffn, w_router_group, b_router_group, w_router_expert,
              b_router_expert, w_gate_up, w_down, norm_final):
    B, S, D = x.shape
    r = jnp.concatenate([jnp.broadcast_to(meta.astype(x.dtype)[None], (B, N_META, D)), x], axis=1)
    L = S + N_META
    cuts = [ATT_W, 2 * ATT_W, 3 * ATT_W, 3 * ATT_W + N_HEADS,
            3 * ATT_W + N_HEADS + 2 * CONV_C, 3 * ATT_W + N_HEADS + 2 * CONV_C + D_MODEL]
    for l in range(DEPTH):
        n = rmsnorm(r, norm_mix[l])
        z = n @ w_in[l]
        q, k, v, f_logit, u, g_att, g_conv = jnp.split(z, cuts, axis=-1)
        log_f = jax.nn.log_sigmoid(f_logit.astype(jnp.float32) + b_forget[l].astype(jnp.float32))
        att = forgetting_attention(q.reshape(B, L, N_HEADS, HEAD_DIM),
                                   k.reshape(B, L, N_HEADS, HEAD_DIM),
                                   v.reshape(B, L, N_HEADS, HEAD_DIM), log_f)
        br_att = att.reshape(B, L, ATT_W) @ w_attn_out[l]
        br_conv = conformer_conv(u, conv_w[l], conv_b[l], conv_ln_g[l], conv_ln_b[l]) @ w_conv_out[l]
        merged = jax.nn.sigmoid(g_att) * br_att + jax.nn.sigmoid(g_conv) * br_conv
        r = r + merged @ w_out[l]
        n = rmsnorm(r, norm_ffn[l])
        r = r + hierarchical_moe(n, w_router_group[l], b_router_group[l], w_router_expert[l],
                                 b_router_expert[l], w_gate_up[l], w_down[l])
    out = rmsnorm(r, norm_final)
    return out[:, N_META:]
```

```python
import functools

import jax
import jax.numpy as jnp
from jax import lax
from jax.experimental import pallas as pl
from jax.experimental.pallas import tpu as pltpu

F32 = jnp.float32
BF16 = jnp.bfloat16
I32 = jnp.int32

LANES = 128
MXU_DIM = 256
VMEM_BYTES_V7X = 64 * 1024 * 1024

N_META = 16
HEAD_DIM = 64
HEADS_PER_BLOCK = LANES // HEAD_DIM
CONV_K = 31
N_GROUPS = 4
EXPERTS_PER_GROUP = 8
N_EXPERTS = N_GROUPS * EXPERTS_PER_GROUP
RMS_EPS = 1e-6
LN_EPS = 1e-5

SEQ_TILE = 256
P0 = SEQ_TILE
META0 = P0 - N_META
INPROJ_TM = 1024
INPROJ_TN = 512
MOE_TM = 256
CONV_HALO = 32
ROUTE_FIRST_EXPERT_LANE = N_GROUPS
MASK_VALUE = -1e30


def _vmem_limit(nbytes):
    return int(min(max(nbytes, 16 * 1024 * 1024), VMEM_BYTES_V7X - 6 * 1024 * 1024))


def _sigmoid(x):
    return 1.0 / (1.0 + jnp.exp(-x))


def _inproj_kernel(r_ref, g_ref, w_ref, wft_ref, z_ref, ft_ref, n_sc):
    @pl.when(pl.program_id(1) == 0)
    def _():
        x = r_ref[...]
        ms = jnp.mean(x * x, axis=-1, keepdims=True)
        n = (x * lax.rsqrt(ms + RMS_EPS) * g_ref[...]).astype(BF16)
        n_sc[...] = n
        ft_ref[...] = lax.dot_general(wft_ref[...], n, (((1,), (1,)), ((), ())),
                                      preferred_element_type=F32)

    z_ref[...] = jnp.dot(n_sc[...], w_ref[...], preferred_element_type=F32).astype(z_ref.dtype)


def _inproj(r, gain, w_main, wft):
    n_tok, d = r.shape
    cols = w_main.shape[1]
    heads = wft.shape[0]
    tm, tn = INPROJ_TM, INPROJ_TN
    vmem = 2 * tm * d * 4 + 2 * d * tn * 2 + 2 * tm * tn * 2 + tm * d * 2 + 4 * heads * (d + tm) * 4
    return pl.pallas_call(
        _inproj_kernel,
        grid=(n_tok // tm, cols // tn),
        in_specs=[
            pl.BlockSpec((tm, d), lambda i, j: (i, 0)),
            pl.BlockSpec((1, d), lambda i, j: (0, 0)),
            pl.BlockSpec((d, tn), lambda i, j: (0, j)),
            pl.BlockSpec((heads, d), lambda i, j: (0, 0)),
        ],
        out_specs=[
            pl.BlockSpec((tm, tn), lambda i, j: (i, j)),
            pl.BlockSpec((heads, tm), lambda i, j: (0, i)),
        ],
        out_shape=[
            jax.ShapeDtypeStruct((n_tok, cols), BF16),
            jax.ShapeDtypeStruct((heads, n_tok), F32),
        ],
        scratch_shapes=[pltpu.VMEM((tm, d), BF16)],
        compiler_params=pltpu.CompilerParams(
            dimension_semantics=("parallel", "arbitrary"),
            vmem_limit_bytes=_vmem_limit(vmem + (8 << 20))),
        name="inproj",
    )(r, gain, w_main, wft)


def _cum_kernel(ft_ref, b_ref, o_ref):
    ch = MXU_DIM
    lp = ft_ref.shape[1]
    row = lax.broadcasted_iota(I32, (ch, ch), 0)
    col = lax.broadcasted_iota(I32, (ch, ch), 1)
    upper = (row <= col).astype(F32)
    carry = jnp.zeros((ft_ref.shape[0], 1), F32)
    for c in range(lp // ch):
        x = ft_ref[:, c * ch:(c + 1) * ch] + b_ref[...]
        log_f = jnp.minimum(x, 0.0) - jnp.log1p(jnp.exp(-jnp.abs(x)))
        loc = jnp.dot(log_f, upper, preferred_element_type=F32,
                      precision=lax.Precision.HIGHEST) + carry
        o_ref[0, :, c * ch:(c + 1) * ch] = loc
        carry = loc[:, ch - 1:ch]


def _forget_cumsum(ft, b_forget, batch, lp):
    heads = ft.shape[0]
    return pl.pallas_call(
        _cum_kernel,
        grid=(batch,),
        in_specs=[
            pl.BlockSpec((heads, lp), lambda b: (0, b)),
            pl.BlockSpec((heads, 1), lambda b: (0, 0)),
        ],
        out_specs=pl.BlockSpec((1, heads, lp), lambda b: (b, 0, 0)),
        out_shape=jax.ShapeDtypeStruct((batch, heads, lp), F32),
        compiler_params=pltpu.CompilerParams(dimension_semantics=("parallel",)),
        name="forget_cumsum",
    )(ft, b_forget)


def _attn_kernel(q_ref, k_ref, v_ref, c_ref, o_ref):
    tq = q_ref.shape[1]
    tk = tq
    hp = pl.program_id(1)
    qi = pl.program_id(2)
    scale = HEAD_DIM ** -0.5

    for hh in range(HEADS_PER_BLOCK):
        lo, hi = hh * HEAD_DIM, (hh + 1) * HEAD_DIM
        qh = q_ref[0, :, lo:hi]
        head = hp * HEADS_PER_BLOCK + hh

        def step(j, carry, masked):
            m, l, acc = carry
            s0 = j * tk if isinstance(j, int) else pl.multiple_of(j * tk, tk)
            kh = k_ref[0, pl.ds(s0, tk), lo:hi]
            vh = v_ref[0, pl.ds(s0, tk), lo:hi]
            s = lax.dot_general(qh, kh, (((1,), (1,)), ((), ())),
                                preferred_element_type=F32) * scale
            s = s - c_ref[0, pl.ds(head, 1), pl.ds(s0, tk)]
            if masked:
                rows = qi * tq + lax.broadcasted_iota(I32, (tq, tk), 0)
                cols = s0 + lax.broadcasted_iota(I32, (tq, tk), 1)
                s = jnp.where(cols <= rows, s, MASK_VALUE)
                s = jnp.where(cols >= META0, s, MASK_VALUE)
            m_new = jnp.maximum(m, jnp.max(s, axis=-1, keepdims=True))
            alpha = jnp.exp(m - m_new)
            p = jnp.exp(s - m_new)
            l = alpha * l + jnp.sum(p, axis=-1, keepdims=True)
            acc = alpha * acc + jnp.dot(p.astype(BF16), vh, preferred_element_type=F32)
            return m_new, l, acc

        carry = (jnp.full((tq, 1), MASK_VALUE, F32), jnp.zeros((tq, 1), F32),
                 jnp.zeros((tq, HEAD_DIM), F32))
        carry = step(0, carry, True)
        carry = lax.fori_loop(1, qi, functools.partial(step, masked=False), carry)
        carry = lax.fori_loop(jnp.maximum(qi, 1), qi + 1, functools.partial(step, masked=True), carry)
        _, l, acc = carry
        o_ref[0, :, lo:hi] = (acc / l).astype(o_ref.dtype)


def _attention(z3, cum, col_q, col_k, col_v):
    batch, lp, _ = z3.shape
    att_w = col_k - col_q
    nblk = att_w // LANES
    tq = SEQ_TILE
    qb, kb, vb = col_q // LANES, col_k // LANES, col_v // LANES
    return pl.pallas_call(
        _attn_kernel,
        grid=(batch, nblk, lp // tq),
        in_specs=[
            pl.BlockSpec((1, tq, LANES), lambda b, h, i: (b, i, qb + h)),
            pl.BlockSpec((1, lp, LANES), lambda b, h, i: (b, 0, kb + h)),
            pl.BlockSpec((1, lp, LANES), lambda b, h, i: (b, 0, vb + h)),
            pl.BlockSpec((1, cum.shape[1], lp), lambda b, h, i: (b, 0, 0)),
        ],
        out_specs=pl.BlockSpec((1, tq, LANES), lambda b, h, i: (b, i, h)),
        out_shape=jax.ShapeDtypeStruct((batch, lp, att_w), BF16),
        compiler_params=pltpu.CompilerParams(
            dimension_semantics=("parallel", "parallel", "arbitrary")),
        name="fox_attention",
    )(z3, z3, z3, cum)


def _conv_kernel(a_ref, b_ref, w_ref, cb_ref, g_ref, lb_ref, o_ref, zbuf, ybuf):
    tt = a_ref.shape[1]
    ch = a_ref.shape[2]
    t = pl.program_id(1)

    @pl.when(t == 0)
    def _():
        zbuf[0:CONV_HALO, :] = jnp.zeros((CONV_HALO, ch), F32)

    z = a_ref[0].astype(F32) * _sigmoid(b_ref[0].astype(F32))
    rows = t * tt + lax.broadcasted_iota(I32, (tt, 1), 0)
    zbuf[CONV_HALO:CONV_HALO + tt, :] = jnp.where(rows >= META0, z, 0.0)

    first = CONV_HALO - (CONV_K - 1)

    def chan_block(cb, _):
        c0 = pl.multiple_of(cb * LANES, LANES)
        acc = jnp.zeros((tt, LANES), F32) + cb_ref[:, pl.ds(c0, LANES)]
        for k in range(CONV_K):
            acc = acc + w_ref[k:k + 1, pl.ds(c0, LANES)] * zbuf[first + k:first + k + tt, pl.ds(c0, LANES)]
        ybuf[:, pl.ds(c0, LANES)] = acc
        return 0

    lax.fori_loop(0, ch // LANES, chan_block, 0)

    y = ybuf[...]
    mu = jnp.mean(y, axis=-1, keepdims=True)
    yc = y - mu
    var = jnp.mean(yc * yc, axis=-1, keepdims=True)
    zn = yc * lax.rsqrt(var + LN_EPS) * g_ref[...] + lb_ref[...]
    o_ref[0] = (zn * _sigmoid(zn)).astype(o_ref.dtype)
    zbuf[0:CONV_HALO, :] = zbuf[tt:tt + CONV_HALO, :]


def _conv_branch(z3, col_a, col_b, conv_w, conv_b, ln_g, ln_b):
    batch, lp, _ = z3.shape
    ch = conv_b.shape[1]
    tt = SEQ_TILE
    ab, bb = col_a // ch, col_b // ch
    return pl.pallas_call(
        _conv_kernel,
        grid=(batch, lp // tt),
        in_specs=[
            pl.BlockSpec((1, tt, ch), lambda b, t: (b, t, ab)),
            pl.BlockSpec((1, tt, ch), lambda b, t: (b, t, bb)),
            pl.BlockSpec(conv_w.shape, lambda b, t: (0, 0)),
            pl.BlockSpec((1, ch), lambda b, t: (0, 0)),
            pl.BlockSpec((1, ch), lambda b, t: (0, 0)),
            pl.BlockSpec((1, ch), lambda b, t: (0, 0)),
        ],
        out_specs=pl.BlockSpec((1, tt, ch), lambda b, t: (b, t, 0)),
        out_shape=jax.ShapeDtypeStruct((batch, lp, ch), BF16),
        scratch_shapes=[pltpu.VMEM((tt + CONV_HALO, ch), F32), pltpu.VMEM((tt, ch), F32)],
        compiler_params=pltpu.CompilerParams(dimension_semantics=("parallel", "arbitrary")),
        name="conv_branch",
    )(z3, z3, conv_w, conv_b, ln_g, ln_b)


def _merge_route_kernel(att_ref, cv_ref, ga_ref, gc_ref, r_ref, wao_ref, wco_ref, wo_ref,
                        g_ref, wr_ref, br_ref, ro_ref, n_ref, route_ref, cnt_ref, cnt_sc,
                        *, tiles_per_seq):
    i = pl.program_id(0)
    tm = r_ref.shape[0]

    @pl.when(i == 0)
    def _():
        cnt_sc[...] = jnp.zeros_like(cnt_sc)

    br_att = jnp.dot(att_ref[...], wao_ref[...], preferred_element_type=F32)
    br_conv = jnp.dot(cv_ref[...], wco_ref[...], preferred_element_type=F32)
    merged = _sigmoid(ga_ref[...].astype(F32)) * br_att + _sigmoid(gc_ref[...].astype(F32)) * br_conv
    r_new = r_ref[...] + jnp.dot(merged.astype(BF16), wo_ref[...], preferred_element_type=F32)
    ro_ref[...] = r_new

    ms = jnp.mean(r_new * r_new, axis=-1, keepdims=True)
    n = r_new * lax.rsqrt(ms + RMS_EPS) * g_ref[...]
    n_ref[...] = n

    logits = jnp.dot(n.astype(BF16), wr_ref[...], preferred_element_type=F32) + br_ref[...]
    lane = lax.broadcasted_iota(I32, logits.shape, 1)
    big = jnp.int32(4 * LANES)
    first_e = ROUTE_FIRST_EXPERT_LANE

    gl = jnp.where(lane < N_GROUPS, logits, -jnp.inf)
    gmax = jnp.max(gl, axis=-1, keepdims=True)
    gsum = jnp.sum(jnp.exp(gl - gmax), axis=-1, keepdims=True)
    g_w = 1.0 / gsum
    g_idx = jnp.min(jnp.where(gl == gmax, lane, big), axis=-1, keepdims=True)

    in_group = (lane >= first_e + g_idx * EXPERTS_PER_GROUP) & \
               (lane < first_e + (g_idx + 1) * EXPERTS_PER_GROUP)
    el = jnp.where(in_group, logits, -jnp.inf)
    v0 = jnp.max(el, axis=-1, keepdims=True)
    i0 = jnp.min(jnp.where(el == v0, lane, big), axis=-1, keepdims=True)
    el = jnp.where(lane == i0, -jnp.inf, el)
    v1 = jnp.max(el, axis=-1, keepdims=True)
    i1 = jnp.min(jnp.where(el == v1, lane, big), axis=-1, keepdims=True)
    e1 = jnp.exp(v1 - v0)
    w0 = g_w / (1.0 + e1)
    w1 = g_w * e1 / (1.0 + e1)

    pos = (i % tiles_per_seq) * tm + lax.broadcasted_iota(I32, (tm, 1), 0)
    valid = pos >= META0
    onehot = jnp.where(((lane == i0) | (lane == i1)) & valid, 1.0, 0.0)
    rr = lax.broadcasted_iota(I32, (tm, tm), 0)
    cc = lax.broadcasted_iota(I32, (tm, tm), 1)
    lower = jnp.where(cc < rr, 1.0, 0.0).astype(BF16)
    rank = jnp.dot(lower, onehot.astype(BF16), preferred_element_type=F32) + cnt_sc[0:1, :]
    rank0 = jnp.sum(jnp.where(lane == i0, rank, 0.0), axis=-1, keepdims=True)
    rank1 = jnp.sum(jnp.where(lane == i1, rank, 0.0), axis=-1, keepdims=True)
    cnt_sc[0:1, :] = cnt_sc[0:1, :] + jnp.sum(onehot, axis=0, keepdims=True)
    cnt_ref[...] = jnp.broadcast_to(cnt_sc[0:1, :], cnt_ref.shape)

    w0 = jnp.where(valid, w0, 0.0)
    w1 = jnp.where(valid, w1, 0.0)
    out = jnp.where(lane == 0, (i0 - first_e).astype(F32), 0.0)
    out = jnp.where(lane == 1, (i1 - first_e).astype(F32), out)
    out = jnp.where(lane == 2, w0, out)
    out = jnp.where(lane == 3, w1, out)
    out = jnp.where(lane == 4, rank0, out)
    out = jnp.where(lane == 5, rank1, out)
    route_ref[...] = out


def _merge_route(att, cv, z, col_ga, col_gc, r, wao, wco, wo, gain, wr, br, tiles_per_seq):
    n_tok, d = r.shape
    aw = att.shape[1]
    cw = cv.shape[1]
    tm = SEQ_TILE
    gab, gcb = col_ga // d, col_gc // d
    const = lambda i: (0, 0)
    single = pl.Buffered(1)
    vmem = (2 * tm * (aw * 2 + cw * 2 + 2 * d * 2 + 3 * d * 4 + LANES * 4)
            + (aw * d + cw * d + d * d + d * LANES) * 2 + 8 * tm * d * 4)
    return pl.pallas_call(
        functools.partial(_merge_route_kernel, tiles_per_seq=tiles_per_seq),
        grid=(n_tok // tm,),
        in_specs=[
            pl.BlockSpec((tm, aw), lambda i: (i, 0)),
            pl.BlockSpec((tm, cw), lambda i: (i, 0)),
            pl.BlockSpec((tm, d), lambda i: (i, gab)),
            pl.BlockSpec((tm, d), lambda i: (i, gcb)),
            pl.BlockSpec((tm, d), lambda i: (i, 0)),
            pl.BlockSpec(wao.shape, const, pipeline_mode=single),
            pl.BlockSpec(wco.shape, const, pipeline_mode=single),
            pl.BlockSpec(wo.shape, const, pipeline_mode=single),
            pl.BlockSpec((1, d), const),
            pl.BlockSpec(wr.shape, const, pipeline_mode=single),
            pl.BlockSpec((1, LANES), const),
        ],
        out_specs=[
            pl.BlockSpec((tm, d), lambda i: (i, 0)),
            pl.BlockSpec((tm, d), lambda i: (i, 0)),
            pl.BlockSpec((tm, LANES), lambda i: (i, 0)),
            pl.BlockSpec((8, LANES), const),
        ],
        out_shape=[
            jax.ShapeDtypeStruct((n_tok, d), F32),
            jax.ShapeDtypeStruct((n_tok, d), F32),
            jax.ShapeDtypeStruct((n_tok, LANES), F32),
            jax.ShapeDtypeStruct((8, LANES), F32),
        ],
        scratch_shapes=[pltpu.VMEM((8, LANES), F32)],
        compiler_params=pltpu.CompilerParams(
            dimension_semantics=("arbitrary",), vmem_limit_bytes=_vmem_limit(vmem)),
        name="merge_route",
    )(att, cv, z, z, r, wao, wco, wo, gain, wr, br)


def _dispatch_kernel(dest_ref, zflag_ref, n_ref, xs_ref, zero_sc, sem, zsem, *, tiles_per_seq):
    i = pl.program_id(0)
    tm = n_ref.shape[0]
    n_tiles = xs_ref.shape[0] // tm

    def zero_copy(t):
        return pltpu.make_async_copy(zero_sc, xs_ref.at[pl.ds(pl.multiple_of(t * tm, tm), tm)], zsem)

    @pl.when(i == 0)
    def _():
        zero_sc[...] = jnp.zeros_like(zero_sc)

        def issue_zero(t, _):
            @pl.when(zflag_ref[t] != 0)
            def _():
                zero_copy(t).start()
            return 0

        def drain_zero(t, _):
            @pl.when(zflag_ref[t] != 0)
            def _():
                zero_copy(t).wait()
            return 0

        lax.fori_loop(0, n_tiles, issue_zero, 0)
        lax.fori_loop(0, n_tiles, drain_zero, 0)

    def row_copy(t, k):
        d = dest_ref[(i * tm + t) * 2 + k]
        return pltpu.make_async_copy(n_ref.at[pl.ds(t, 1)], xs_ref.at[pl.ds(d, 1)], sem)

    start = jnp.where(i % tiles_per_seq == 0, META0, 0)

    def issue(t, _):
        row_copy(t, 0).start()
        row_copy(t, 1).start()
        return 0

    def drain(t, _):
        row_copy(t, 0).wait()
        row_copy(t, 1).wait()
        return 0

    lax.fori_loop(start, tm, issue, 0)
    lax.fori_loop(start, tm, drain, 0)


def _dispatch(dest, zflag, n, n_rows, tiles_per_seq):
    n_tok, d = n.shape
    tm = MOE_TM
    return pl.pallas_call(
        functools.partial(_dispatch_kernel, tiles_per_seq=tiles_per_seq),
        grid_spec=pltpu.PrefetchScalarGridSpec(
            num_scalar_prefetch=2,
            grid=(n_tok // tm,),
            in_specs=[pl.BlockSpec((tm, d), lambda i, *_: (i, 0))],
            out_specs=pl.BlockSpec(memory_space=pl.ANY),
            scratch_shapes=[pltpu.VMEM((tm, d), F32), pltpu.SemaphoreType.DMA, pltpu.SemaphoreType.DMA],
        ),
        out_shape=jax.ShapeDtypeStruct((n_rows, d), F32),
        compiler_params=pltpu.CompilerParams(dimension_semantics=("arbitrary",),
                                             has_side_effects=True),
        name="moe_dispatch",
    )(dest, zflag, n)


def _expert_kernel(te_ref, na_ref, x_ref, wgu_ref, wdn_ref, y_ref):
    active = pl.program_id(0) < na_ref[0]

    @pl.when(active)
    def _():
        de = wdn_ref.shape[0]
        h = jnp.dot(x_ref[...].astype(BF16), wgu_ref[...], preferred_element_type=F32)
        a = h[:, :de]
        b = h[:, de:]
        act = a * _sigmoid(a) * b
        y_ref[...] = jnp.dot(act.astype(BF16), wdn_ref[...], preferred_element_type=F32)

    @pl.when(jnp.logical_not(active))
    def _():
        y_ref[...] = jnp.zeros_like(y_ref)


def _experts(tile_e, n_act, xs, wgu, wdn):
    n_rows, d = xs.shape
    tm = MOE_TM
    de2 = wgu.shape[2]
    de = wdn.shape[1]
    row_map = lambda i, te, na: (jnp.minimum(i, na[0] - 1), 0)
    vmem = 2 * (d * de2 + de * d) * 2 + 4 * tm * d * 4 + 6 * tm * de2 * 4
    return pl.pallas_call(
        _expert_kernel,
        grid_spec=pltpu.PrefetchScalarGridSpec(
            num_scalar_prefetch=2,
            grid=(n_rows // tm,),
            in_specs=[
                pl.BlockSpec((tm, d), row_map),
                pl.BlockSpec((None, d, de2), lambda i, te, na: (te[i], 0, 0)),
                pl.BlockSpec((None, de, d), lambda i, te, na: (te[i], 0, 0)),
            ],
            out_specs=pl.BlockSpec((tm, d), lambda i, te, na: (i, 0)),
        ),
        out_shape=jax.ShapeDtypeStruct((n_rows, d), F32),
        compiler_params=pltpu.CompilerParams(
            dimension_semantics=("arbitrary",), vmem_limit_bytes=_vmem_limit(vmem)),
        name="moe_experts",
    )(tile_e, n_act, xs, wgu, wdn)


def _combine_kernel(dest_ref, r_ref, route_ref, y_ref, *rest, tile_of, final):
    if final:
        g_ref, o_ref, ybuf, sem = rest
    else:
        o_ref, ybuf, sem = rest
    tm = r_ref.shape[0]
    tile = tile_of(pl.program_id(0))

    def row_copy(t, k):
        d = dest_ref[(tile * tm + t) * 2 + k]
        return pltpu.make_async_copy(y_ref.at[pl.ds(d, 1)], ybuf.at[k, pl.ds(t, 1)], sem)

    def issue(t, _):
        row_copy(t, 0).start()
        row_copy(t, 1).start()
        return 0

    def drain(t, _):
        row_copy(t, 0).wait()
        row_copy(t, 1).wait()
        return 0

    lax.fori_loop(0, tm, issue, 0)
    lax.fori_loop(0, tm, drain, 0)

    route = route_ref[...]
    out = r_ref[...] + route[:, 2:3] * ybuf[0] + route[:, 3:4] * ybuf[1]
    if final:
        ms = jnp.mean(out * out, axis=-1, keepdims=True)
        out = out * lax.rsqrt(ms + RMS_EPS) * g_ref[...]
    o_ref[...] = out


def _combine(dest, r, route, y, tiles_per_seq, final_gain=None):
    n_tok, d = r.shape
    tm = MOE_TM
    final = final_gain is not None
    if final:
        real_tiles = tiles_per_seq - P0 // tm
        tile_of = lambda i: (i // real_tiles) * tiles_per_seq + P0 // tm + i % real_tiles
        n_out_tiles = (n_tok // tm // tiles_per_seq) * real_tiles
    else:
        tile_of = lambda i: i
        n_out_tiles = n_tok // tm
    in_specs = [
        pl.BlockSpec((tm, d), lambda i, *_: (tile_of(i), 0)),
        pl.BlockSpec((tm, LANES), lambda i, *_: (tile_of(i), 0)),
        pl.BlockSpec(memory_space=pl.ANY),
    ]
    args = [dest, r, route, y]
    if final:
        in_specs.append(pl.BlockSpec((1, d), lambda i, *_: (0, 0)))
        args.append(final_gain)
    return pl.pallas_call(
        functools.partial(_combine_kernel, tile_of=tile_of, final=final),
        grid_spec=pltpu.PrefetchScalarGridSpec(
            num_scalar_prefetch=1,
            grid=(n_out_tiles,),
            in_specs=in_specs,
            out_specs=pl.BlockSpec((tm, d), lambda i, *_: (i, 0)),
            scratch_shapes=[pltpu.VMEM((2, tm, d), F32), pltpu.SemaphoreType.DMA],
        ),
        out_shape=jax.ShapeDtypeStruct((n_out_tiles * tm, d), F32),
        compiler_params=pltpu.CompilerParams(dimension_semantics=("arbitrary",)),
        name="moe_combine_final" if final else "moe_combine",
    )(*args)


def _routing_tables(route, counts, n_tiles_max, valid_tok):
    tm = MOE_TM
    cnt = counts[0, ROUTE_FIRST_EXPERT_LANE:ROUTE_FIRST_EXPERT_LANE + N_EXPERTS].astype(I32)
    ntile = (cnt + tm - 1) // tm
    tile_end = jnp.cumsum(ntile)
    tile_start = tile_end - ntile
    n_act = tile_end[-1:]
    e_ids = route[:, 0:2].astype(I32)
    rank = route[:, 4:6].astype(I32)
    dest = jnp.where(valid_tok[:, None], tile_start[e_ids] * tm + rank, 0).reshape(-1)
    all_tiles = jnp.arange(n_tiles_max, dtype=I32)
    tile_e = jnp.searchsorted(tile_end, jnp.minimum(all_tiles, n_act[0] - 1), side="right").astype(I32)
    is_expert_tail = jnp.any((all_tiles[:, None] == tile_end[None, :] - 1) & (ntile[None, :] > 0), axis=1)
    zflag = (is_expert_tail | (all_tiles >= n_act[0])).astype(I32)
    return dest, tile_e, n_act.astype(I32), zflag


def kernel(x, meta, norm_mix, w_in, b_forget, w_attn_out, conv_w, conv_b, conv_ln_g, conv_ln_b,
           w_conv_out, w_out, norm_ffn, w_router_group, b_router_group, w_router_expert,
           b_router_expert, w_gate_up, w_down, norm_final):
    batch, seq, d = x.shape
    depth = w_in.shape[0]
    heads = b_forget.shape[1]
    att_w = heads * HEAD_DIM
    conv_c = conv_b.shape[1]
    lp = P0 + seq
    tiles_per_seq = lp // SEQ_TILE
    n_tok = batch * lp
    assert seq % SEQ_TILE == 0 and n_tok % INPROJ_TM == 0 and d % INPROJ_TN == 0
    assert att_w % LANES == 0 and conv_c % LANES == 0 and SEQ_TILE == MOE_TM

    front = jnp.concatenate([jnp.zeros((META0, d), x.dtype), meta.astype(x.dtype)], axis=0)
    r = jnp.concatenate([jnp.broadcast_to(front[None], (batch, P0, d)), x], axis=1).reshape(n_tok, d)

    c_q, c_k, c_v = 0, att_w, 2 * att_w
    c_f = 3 * att_w
    c_u = c_f + heads
    c_ga = c_u + 2 * conv_c
    c_gc = c_ga + d
    m_ga, m_gc = 0, d
    m_q = 2 * d
    m_k, m_v = m_q + att_w, m_q + 2 * att_w
    m_ua = m_q + 3 * att_w
    m_ub = m_ua + conv_c

    valid_tok = jnp.tile(jnp.arange(lp) >= META0, batch)
    n_pairs = 2 * batch * (lp - META0)
    n_tiles_max = n_pairs // MOE_TM + N_EXPERTS
    n_rows = n_tiles_max * MOE_TM

    out = None
    for l in range(depth):
        wl = w_in[l]
        w_main = jnp.concatenate(
            [wl[:, c_ga:c_ga + d], wl[:, c_gc:c_gc + d], wl[:, c_q:c_f], wl[:, c_u:c_ga]],
            axis=1).astype(BF16)
        wft = wl[:, c_f:c_u].T.astype(BF16)
        z, ft = _inproj(r, norm_mix[l][None], w_main, wft)
        cum = _forget_cumsum(ft, b_forget[l][:, None], batch, lp)
        z3 = z.reshape(batch, lp, -1)
        att = _attention(z3, cum, m_q, m_k, m_v).reshape(n_tok, att_w)
        conv_w_pad = jnp.concatenate([conv_w[l], jnp.zeros((1, conv_c), F32)], axis=0)
        cv = _conv_branch(z3, m_ua, m_ub, conv_w_pad, conv_b[l][None], conv_ln_g[l][None],
                          conv_ln_b[l][None]).reshape(n_tok, conv_c)

        wr = jnp.zeros((d, LANES), F32)
        wr = wr.at[:, :N_GROUPS].set(w_router_group[l])
        wr = wr.at[:, N_GROUPS:N_GROUPS + N_EXPERTS].set(w_router_expert[l]).astype(BF16)
        br = jnp.zeros((1, LANES), F32)
        br = br.at[0, :N_GROUPS].set(b_router_group[l])
        br = br.at[0, N_GROUPS:N_GROUPS + N_EXPERTS].set(b_router_expert[l])
        r, n2, route, counts = _merge_route(
            att, cv, z, m_ga, m_gc, r, w_attn_out[l].astype(BF16), w_conv_out[l].astype(BF16),
            w_out[l].astype(BF16), norm_ffn[l][None], wr, br, tiles_per_seq)

        dest, tile_e, n_act, zflag = _routing_tables(route, counts, n_tiles_max, valid_tok)
        xs = _dispatch(dest, zflag, n2, n_rows, tiles_per_seq)
        y = _experts(tile_e, n_act, xs, w_gate_up[l].astype(BF16), w_down[l].astype(BF16))
        if l + 1 < depth:
            r = _combine(dest, r, route, y, tiles_per_seq)
        else:
            out = _combine(dest, r, route, y, tiles_per_seq, final_gain=norm_final[None])
    return out.reshape(batch, seq, d)
```

```python
import functools

import jax
import jax.numpy as jnp
from jax import lax
from jax.experimental import pallas as pl
from jax.experimental.pallas import tpu as pltpu

F32 = jnp.float32
BF16 = jnp.bfloat16
I32 = jnp.int32

LANES = 128
MXU_DIM = 256
VMEM_BYTES_V7X = 64 * 1024 * 1024

N_META = 16
HEAD_DIM = 64
HEADS_PER_BLOCK = LANES // HEAD_DIM
ATT_HEADS = 4
ATT_BLOCK = ATT_HEADS * HEAD_DIM
CONV_K = 31
N_GROUPS = 4
EXPERTS_PER_GROUP = 8
N_EXPERTS = N_GROUPS * EXPERTS_PER_GROUP
RMS_EPS = 1e-6
LN_EPS = 1e-5

SEQ_TILE = 256
P0 = SEQ_TILE
META0 = P0 - N_META
INPROJ_TM = 1024
INPROJ_TN = 512
MOE_TM = 256
CONV_HALO = 32
ROUTE_FIRST_EXPERT_LANE = N_GROUPS
MASK_VALUE = -1e30


def _vmem_limit(nbytes):
    return int(min(max(nbytes, 16 * 1024 * 1024), VMEM_BYTES_V7X - 6 * 1024 * 1024))


def _sigmoid(x):
    return 1.0 / (1.0 + jnp.exp(-x))


def _inproj_kernel(r_ref, g_ref, w_ref, wf_ref, z_ref, f_ref, n_sc):
    @pl.when(pl.program_id(1) == 0)
    def _():
        x = r_ref[...]
        ms = jnp.mean(x * x, axis=-1, keepdims=True)
        n = (x * lax.rsqrt(ms + RMS_EPS) * g_ref[...]).astype(BF16)
        n_sc[...] = n
        f_ref[...] = jnp.dot(n, wf_ref[...], preferred_element_type=F32)

    z_ref[...] = jnp.dot(n_sc[...], w_ref[...], preferred_element_type=F32).astype(z_ref.dtype)


def _inproj(r, gain, w_main, wf):
    n_tok, d = r.shape
    cols = w_main.shape[1]
    tm, tn = INPROJ_TM, INPROJ_TN
    vmem = 2 * tm * d * 4 + 2 * d * tn * 2 + 2 * tm * tn * 2 + tm * d * 2 + 2 * (d * 2 + tm * 4) * LANES
    return pl.pallas_call(
        _inproj_kernel,
        grid=(n_tok // tm, cols // tn),
        in_specs=[
            pl.BlockSpec((tm, d), lambda i, j: (i, 0)),
            pl.BlockSpec((1, d), lambda i, j: (0, 0)),
            pl.BlockSpec((d, tn), lambda i, j: (0, j)),
            pl.BlockSpec((d, LANES), lambda i, j: (0, 0)),
        ],
        out_specs=[
            pl.BlockSpec((tm, tn), lambda i, j: (i, j)),
            pl.BlockSpec((tm, LANES), lambda i, j: (i, 0)),
        ],
        out_shape=[
            jax.ShapeDtypeStruct((n_tok, cols), BF16),
            jax.ShapeDtypeStruct((n_tok, LANES), F32),
        ],
        scratch_shapes=[pltpu.VMEM((tm, d), BF16)],
        compiler_params=pltpu.CompilerParams(
            dimension_semantics=("parallel", "arbitrary"),
            vmem_limit_bytes=_vmem_limit(vmem + (8 << 20))),
        name="inproj",
    )(r, gain, w_main, wf)


def _cum_kernel(f_ref, b_ref, o_ref):
    ch = MXU_DIM
    lp = f_ref.shape[1]
    row = lax.broadcasted_iota(I32, (ch, ch), 0)
    col = lax.broadcasted_iota(I32, (ch, ch), 1)
    lower = (col <= row).astype(F32)
    carry = jnp.zeros((1, LANES), F32)
    for c in range(lp // ch):
        x = f_ref[0, c * ch:(c + 1) * ch, :] + b_ref[...]
        log_f = jnp.minimum(x, 0.0) - jnp.log1p(jnp.exp(-jnp.abs(x)))
        loc = jnp.dot(lower, log_f, preferred_element_type=F32,
                      precision=lax.Precision.HIGHEST) + carry
        o_ref[0, c * ch:(c + 1) * ch, :] = loc
        carry = loc[ch - 1:ch, :]


def _forget_cumsum(f3, b_forget):
    batch, lp, _ = f3.shape
    return pl.pallas_call(
        _cum_kernel,
        grid=(batch,),
        in_specs=[
            pl.BlockSpec((1, lp, LANES), lambda b: (b, 0, 0)),
            pl.BlockSpec((1, LANES), lambda b: (0, 0)),
        ],
        out_specs=pl.BlockSpec((1, lp, LANES), lambda b: (b, 0, 0)),
        out_shape=jax.ShapeDtypeStruct((batch, lp, LANES), F32),
        compiler_params=pltpu.CompilerParams(dimension_semantics=("parallel",)),
        name="forget_cumsum",
    )(f3, b_forget)


def _attn_kernel(q_ref, k_ref, v_ref, c_ref, o_ref, kaug_sc, vt_sc):
    tq = q_ref.shape[1]
    tk = tq
    lp = k_ref.shape[1]
    hg = pl.program_id(1)
    qi = pl.program_id(2)
    lane = lax.broadcasted_iota(I32, (1, LANES), 1)
    nt_dims = (((1,), (1,)), ((), ()))

    def own_lanes(h):
        return lane < HEAD_DIM if h % HEADS_PER_BLOCK == 0 else lane >= HEAD_DIM

    def bias_lane(h):
        return HEAD_DIM if h % HEADS_PER_BLOCK == 0 else 0

    def block_lanes(h):
        blk = h // HEADS_PER_BLOCK
        return slice(blk * LANES, (blk + 1) * LANES)

    @pl.when(qi == 0)
    def _():
        for c in range(lp // tk):
            rows = slice(c * tk, (c + 1) * tk)
            vt_sc[:, rows] = v_ref[0, rows, :].astype(F32).T.astype(BF16)
            cum = c_ref[0, rows, :]
            key_pos = c * tk + lax.broadcasted_iota(I32, (tk, 1), 0)
            for h in range(ATT_HEADS):
                head = hg * ATT_HEADS + h
                col = jnp.sum(jnp.where(lane == head, cum, 0.0), axis=-1, keepdims=True)
                hi = col.astype(BF16).astype(F32)
                rem = col - hi
                mid = rem.astype(BF16).astype(F32)
                low = rem - mid
                hi = jnp.where(key_pos < META0, -MASK_VALUE, hi)
                a = bias_lane(h)
                bias = jnp.where(lane == a, hi, jnp.where(lane == a + 1, mid,
                                                          jnp.where(lane == a + 2, low, 0.0)))
                kaug_sc[h, rows, :] = jnp.where(own_lanes(h), k_ref[0, rows, block_lanes(h)],
                                                bias.astype(BF16))

    q_aug = []
    for h in range(ATT_HEADS):
        a = bias_lane(h)
        minus_one = jnp.where((lane >= a) & (lane < a + 3), -1.0, 0.0).astype(BF16)
        scaled = q_ref[0, :, block_lanes(h)] * jnp.asarray(HEAD_DIM ** -0.5, BF16)
        q_aug.append(jnp.where(own_lanes(h), scaled, minus_one))

    def scores_of(j):
        s0 = pl.multiple_of(j * tk, tk)
        return tuple(lax.dot_general(kaug_sc[h, pl.ds(s0, tk), :], q_aug[h], nt_dims,
                                     preferred_element_type=F32) for h in range(ATT_HEADS))

    def softmax_pv(j, scores, carry, diagonal):
        s0 = pl.multiple_of(j * tk, tk)
        out = []
        for h in range(ATT_HEADS):
            m, l, acc = carry[h]
            st = scores[h]
            if diagonal:
                key = lax.broadcasted_iota(I32, (tk, tq), 0)
                qry = lax.broadcasted_iota(I32, (tk, tq), 1)
                st = jnp.where(key <= qry, st, MASK_VALUE)
            m_new = jnp.maximum(m, jnp.max(st, axis=0, keepdims=True))
            alpha = jnp.exp(m - m_new)
            p = jnp.exp(st - m_new)
            l = alpha * l + jnp.sum(p, axis=0, keepdims=True)
            pv = jnp.dot(vt_sc[h * HEAD_DIM:(h + 1) * HEAD_DIM, pl.ds(s0, tk)], p.astype(BF16),
                         preferred_element_type=F32)
            out.append((m_new, l, alpha * acc + pv))
        return tuple(out)

    init = tuple((jnp.full((1, tq), MASK_VALUE, F32), jnp.zeros((1, tq), F32),
                  jnp.zeros((HEAD_DIM, tq), F32)) for _ in range(ATT_HEADS))
    def body(j, carry):
        return softmax_pv(j, scores_of(j), carry, False)

    carry = lax.fori_loop(0, qi, body, init)
    carry = softmax_pv(qi, scores_of(qi), carry, True)
    att_t = jnp.concatenate([acc / l for _, l, acc in carry], axis=0)
    o_ref[0] = att_t.T.astype(o_ref.dtype)


def _attention(z3, cum, col_q, col_k, col_v):
    batch, lp, _ = z3.shape
    att_w = col_k - col_q
    tq = SEQ_TILE
    qb, kb, vb = col_q // ATT_BLOCK, col_k // ATT_BLOCK, col_v // ATT_BLOCK
    return pl.pallas_call(
        _attn_kernel,
        grid=(batch, att_w // ATT_BLOCK, lp // tq),
        in_specs=[
            pl.BlockSpec((1, tq, ATT_BLOCK), lambda b, h, i: (b, i, qb + h)),
            pl.BlockSpec((1, lp, ATT_BLOCK), lambda b, h, i: (b, 0, kb + h)),
            pl.BlockSpec((1, lp, ATT_BLOCK), lambda b, h, i: (b, 0, vb + h)),
            pl.BlockSpec((1, lp, LANES), lambda b, h, i: (b, 0, 0)),
        ],
        out_specs=pl.BlockSpec((1, tq, ATT_BLOCK), lambda b, h, i: (b, i, h)),
        out_shape=jax.ShapeDtypeStruct((batch, lp, att_w), BF16),
        scratch_shapes=[pltpu.VMEM((ATT_HEADS, lp, LANES), BF16), pltpu.VMEM((ATT_BLOCK, lp), BF16)],
        compiler_params=pltpu.CompilerParams(
            dimension_semantics=("parallel", "parallel", "arbitrary")),
        name="fox_attention",
    )(z3, z3, z3, cum)


def _conv_kernel(a_ref, b_ref, w_ref, cb_ref, g_ref, lb_ref, o_ref, zbuf, ybuf):
    tt = a_ref.shape[1]
    ch = a_ref.shape[2]
    t = pl.program_id(1)

    @pl.when(t == 0)
    def _():
        zbuf[0:CONV_HALO, :] = jnp.zeros((CONV_HALO, ch), F32)

    z = a_ref[0].astype(F32) * _sigmoid(b_ref[0].astype(F32))
    rows = t * tt + lax.broadcasted_iota(I32, (tt, 1), 0)
    zbuf[CONV_HALO:CONV_HALO + tt, :] = jnp.where(rows >= META0, z, 0.0)

    first = CONV_HALO - (CONV_K - 1)

    def chan_block(cb, _):
        c0 = pl.multiple_of(cb * LANES, LANES)
        acc = jnp.zeros((tt, LANES), F32) + cb_ref[:, pl.ds(c0, LANES)]
        for k in range(CONV_K):
            acc = acc + w_ref[k:k + 1, pl.ds(c0, LANES)] * zbuf[first + k:first + k + tt, pl.ds(c0, LANES)]
        ybuf[:, pl.ds(c0, LANES)] = acc
        return 0

    lax.fori_loop(0, ch // LANES, chan_block, 0)

    y = ybuf[...]
    mu = jnp.mean(y, axis=-1, keepdims=True)
    yc = y - mu
    var = jnp.mean(yc * yc, axis=-1, keepdims=True)
    zn = yc * lax.rsqrt(var + LN_EPS) * g_ref[...] + lb_ref[...]
    o_ref[0] = (zn * _sigmoid(zn)).astype(o_ref.dtype)
    zbuf[0:CONV_HALO, :] = zbuf[tt:tt + CONV_HALO, :]


def _conv_branch(z3, col_a, col_b, conv_w, conv_b, ln_g, ln_b):
    batch, lp, _ = z3.shape
    ch = conv_b.shape[1]
    tt = SEQ_TILE
    ab, bb = col_a // ch, col_b // ch
    return pl.pallas_call(
        _conv_kernel,
        grid=(batch, lp // tt),
        in_specs=[
            pl.BlockSpec((1, tt, ch), lambda b, t: (b, t, ab)),
            pl.BlockSpec((1, tt, ch), lambda b, t: (b, t, bb)),
            pl.BlockSpec(conv_w.shape, lambda b, t: (0, 0)),
            pl.BlockSpec((1, ch), lambda b, t: (0, 0)),
            pl.BlockSpec((1, ch), lambda b, t: (0, 0)),
            pl.BlockSpec((1, ch), lambda b, t: (0, 0)),
        ],
        out_specs=pl.BlockSpec((1, tt, ch), lambda b, t: (b, t, 0)),
        out_shape=jax.ShapeDtypeStruct((batch, lp, ch), BF16),
        scratch_shapes=[pltpu.VMEM((tt + CONV_HALO, ch), F32), pltpu.VMEM((tt, ch), F32)],
        compiler_params=pltpu.CompilerParams(dimension_semantics=("parallel", "arbitrary")),
        name="conv_branch",
    )(z3, z3, conv_w, conv_b, ln_g, ln_b)


def _merge_route_kernel(att_ref, cv_ref, ga_ref, gc_ref, r_ref, wao_ref, wco_ref, wo_ref,
                        g_ref, wr_ref, br_ref, ro_ref, n_ref, route_ref, cnt_ref, cnt_sc,
                        *, tiles_per_seq):
    i = pl.program_id(0)
    tm = r_ref.shape[0]

    @pl.when(i == 0)
    def _():
        cnt_sc[...] = jnp.zeros_like(cnt_sc)

    br_att = jnp.dot(att_ref[...], wao_ref[...], preferred_element_type=F32)
    br_conv = jnp.dot(cv_ref[...], wco_ref[...], preferred_element_type=F32)
    merged = _sigmoid(ga_ref[...].astype(F32)) * br_att + _sigmoid(gc_ref[...].astype(F32)) * br_conv
    r_new = r_ref[...] + jnp.dot(merged.astype(BF16), wo_ref[...], preferred_element_type=F32)
    ro_ref[...] = r_new

    ms = jnp.mean(r_new * r_new, axis=-1, keepdims=True)
    n = r_new * lax.rsqrt(ms + RMS_EPS) * g_ref[...]
    n_ref[...] = n

    logits = jnp.dot(n.astype(BF16), wr_ref[...], preferred_element_type=F32) + br_ref[...]
    lane = lax.broadcasted_iota(I32, logits.shape, 1)
    big = jnp.int32(4 * LANES)
    first_e = ROUTE_FIRST_EXPERT_LANE

    gl = jnp.where(lane < N_GROUPS, logits, -jnp.inf)
    gmax = jnp.max(gl, axis=-1, keepdims=True)
    gsum = jnp.sum(jnp.exp(gl - gmax), axis=-1, keepdims=True)
    g_w = 1.0 / gsum
    g_idx = jnp.min(jnp.where(gl == gmax, lane, big), axis=-1, keepdims=True)

    in_group = (lane >= first_e + g_idx * EXPERTS_PER_GROUP) & \
               (lane < first_e + (g_idx + 1) * EXPERTS_PER_GROUP)
    el = jnp.where(in_group, logits, -jnp.inf)
    v0 = jnp.max(el, axis=-1, keepdims=True)
    i0 = jnp.min(jnp.where(el == v0, lane, big), axis=-1, keepdims=True)
    el = jnp.where(lane == i0, -jnp.inf, el)
    v1 = jnp.max(el, axis=-1, keepdims=True)
    i1 = jnp.min(jnp.where(el == v1, lane, big), axis=-1, keepdims=True)
    e1 = jnp.exp(v1 - v0)
    w0 = g_w / (1.0 + e1)
    w1 = g_w * e1 / (1.0 + e1)

    pos = (i % tiles_per_seq) * tm + lax.broadcasted_iota(I32, (tm, 1), 0)
    valid = pos >= META0
    onehot = jnp.where(((lane == i0) | (lane == i1)) & valid, 1.0, 0.0)
    rr = lax.broadcasted_iota(I32, (tm, tm), 0)
    cc = lax.broadcasted_iota(I32, (tm, tm), 1)
    lower = jnp.where(cc < rr, 1.0, 0.0).astype(BF16)
    rank = jnp.dot(lower, onehot.astype(BF16), preferred_element_type=F32) + cnt_sc[0:1, :]
    rank0 = jnp.sum(jnp.where(lane == i0, rank, 0.0), axis=-1, keepdims=True)
    rank1 = jnp.sum(jnp.where(lane == i1, rank, 0.0), axis=-1, keepdims=True)
    cnt_sc[0:1, :] = cnt_sc[0:1, :] + jnp.sum(onehot, axis=0, keepdims=True)
    cnt_ref[...] = jnp.broadcast_to(cnt_sc[0:1, :], cnt_ref.shape)

    w0 = jnp.where(valid, w0, 0.0)
    w1 = jnp.where(valid, w1, 0.0)
    out = jnp.where(lane == 0, (i0 - first_e).astype(F32), 0.0)
    out = jnp.where(lane == 1, (i1 - first_e).astype(F32), out)
    out = jnp.where(lane == 2, w0, out)
    out = jnp.where(lane == 3, w1, out)
    out = jnp.where(lane == 4, rank0, out)
    out = jnp.where(lane == 5, rank1, out)
    route_ref[...] = out


def _merge_route(att, cv, z, col_ga, col_gc, r, wao, wco, wo, gain, wr, br, tiles_per_seq):
    n_tok, d = r.shape
    aw = att.shape[1]
    cw = cv.shape[1]
    tm = SEQ_TILE
    gab, gcb = col_ga // d, col_gc // d
    const = lambda i: (0, 0)
    single = pl.Buffered(1)
    vmem = (2 * tm * (aw * 2 + cw * 2 + 2 * d * 2 + 3 * d * 4 + LANES * 4)
            + (aw * d + cw * d + d * d + d * LANES) * 2 + 8 * tm * d * 4)
    return pl.pallas_call(
        functools.partial(_merge_route_kernel, tiles_per_seq=tiles_per_seq),
        grid=(n_tok // tm,),
        in_specs=[
            pl.BlockSpec((tm, aw), lambda i: (i, 0)),
            pl.BlockSpec((tm, cw), lambda i: (i, 0)),
            pl.BlockSpec((tm, d), lambda i: (i, gab)),
            pl.BlockSpec((tm, d), lambda i: (i, gcb)),
            pl.BlockSpec((tm, d), lambda i: (i, 0)),
            pl.BlockSpec(wao.shape, const, pipeline_mode=single),
            pl.BlockSpec(wco.shape, const, pipeline_mode=single),
            pl.BlockSpec(wo.shape, const, pipeline_mode=single),
            pl.BlockSpec((1, d), const),
            pl.BlockSpec(wr.shape, const, pipeline_mode=single),
            pl.BlockSpec((1, LANES), const),
        ],
        out_specs=[
            pl.BlockSpec((tm, d), lambda i: (i, 0)),
            pl.BlockSpec((tm, d), lambda i: (i, 0)),
            pl.BlockSpec((tm, LANES), lambda i: (i, 0)),
            pl.BlockSpec((8, LANES), const),
        ],
        out_shape=[
            jax.ShapeDtypeStruct((n_tok, d), F32),
            jax.ShapeDtypeStruct((n_tok, d), F32),
            jax.ShapeDtypeStruct((n_tok, LANES), F32),
            jax.ShapeDtypeStruct((8, LANES), F32),
        ],
        scratch_shapes=[pltpu.VMEM((8, LANES), F32)],
        compiler_params=pltpu.CompilerParams(
            dimension_semantics=("arbitrary",), vmem_limit_bytes=_vmem_limit(vmem)),
        name="merge_route",
    )(att, cv, z, z, r, wao, wco, wo, gain, wr, br)


def _dispatch_kernel(dest_ref, zflag_ref, n_ref, xs_ref, zero_sc, sem, zsem, *, tiles_per_seq):
    i = pl.program_id(0)
    tm = n_ref.shape[0]
    n_tiles = xs_ref.shape[0] // tm

    def zero_copy(t):
        return pltpu.make_async_copy(zero_sc, xs_ref.at[pl.ds(pl.multiple_of(t * tm, tm), tm)], zsem)

    @pl.when(i == 0)
    def _():
        zero_sc[...] = jnp.zeros_like(zero_sc)

        def issue_zero(t, _):
            @pl.when(zflag_ref[t] != 0)
            def _():
                zero_copy(t).start()
            return 0

        def drain_zero(t, _):
            @pl.when(zflag_ref[t] != 0)
            def _():
                zero_copy(t).wait()
            return 0

        lax.fori_loop(0, n_tiles, issue_zero, 0)
        lax.fori_loop(0, n_tiles, drain_zero, 0)

    def row_copy(t, k):
        d = dest_ref[(i * tm + t) * 2 + k]
        return pltpu.make_async_copy(n_ref.at[pl.ds(t, 1)], xs_ref.at[pl.ds(d, 1)], sem)

    start = jnp.where(i % tiles_per_seq == 0, META0, 0)

    def issue(t, _):
        row_copy(t, 0).start()
        row_copy(t, 1).start()
        return 0

    def drain(t, _):
        row_copy(t, 0).wait()
        row_copy(t, 1).wait()
        return 0

    lax.fori_loop(start, tm, issue, 0)
    lax.fori_loop(start, tm, drain, 0)


def _dispatch(dest, zflag, n, n_rows, tiles_per_seq):
    n_tok, d = n.shape
    tm = MOE_TM
    return pl.pallas_call(
        functools.partial(_dispatch_kernel, tiles_per_seq=tiles_per_seq),
        grid_spec=pltpu.PrefetchScalarGridSpec(
            num_scalar_prefetch=2,
            grid=(n_tok // tm,),
            in_specs=[pl.BlockSpec((tm, d), lambda i, *_: (i, 0))],
            out_specs=pl.BlockSpec(memory_space=pl.ANY),
            scratch_shapes=[pltpu.VMEM((tm, d), F32), pltpu.SemaphoreType.DMA, pltpu.SemaphoreType.DMA],
        ),
        out_shape=jax.ShapeDtypeStruct((n_rows, d), F32),
        compiler_params=pltpu.CompilerParams(dimension_semantics=("arbitrary",),
                                             has_side_effects=True),
        name="moe_dispatch",
    )(dest, zflag, n)


def _expert_kernel(te_ref, first_ref, nxt_ref, na_ref, x_ref, wgu_hbm, wdn_hbm, y_ref,
                   gu_stage, dn_stage, gu_bf, dn_bf, sem, *, layer):
    i = pl.program_id(0)
    active = i < na_ref[0]

    def weight_copies(e):
        return (pltpu.make_async_copy(wgu_hbm.at[layer, e], gu_stage, sem.at[0]),
                pltpu.make_async_copy(wdn_hbm.at[layer, e], dn_stage, sem.at[1]))

    @pl.when(i == 0)
    def _():
        for cp in weight_copies(te_ref[0]):
            cp.start()

    @pl.when(active & (first_ref[i] != 0))
    def _():
        for cp in weight_copies(te_ref[i]):
            cp.wait()
        gu_bf[...] = gu_stage[...].astype(BF16)
        dn_bf[...] = dn_stage[...].astype(BF16)

        @pl.when(nxt_ref[i] >= 0)
        def _():
            for cp in weight_copies(nxt_ref[i]):
                cp.start()

    @pl.when(active)
    def _():
        de = dn_bf.shape[0]
        h = jnp.dot(x_ref[...].astype(BF16), gu_bf[...], preferred_element_type=F32)
        a = h[:, :de]
        b = h[:, de:]
        act = a * _sigmoid(a) * b
        y_ref[...] = jnp.dot(act.astype(BF16), dn_bf[...], preferred_element_type=F32)

    @pl.when(jnp.logical_not(active))
    def _():
        y_ref[...] = jnp.zeros_like(y_ref)


def _experts(tile_e, first, nxt, n_act, xs, w_gate_up, w_down, layer):
    n_rows, d = xs.shape
    tm = MOE_TM
    de2 = w_gate_up.shape[3]
    de = w_down.shape[2]
    row_map = lambda i, te, fi, nx, na: (jnp.minimum(i, na[0] - 1), 0)
    vmem = (d * de2 + de * d) * (4 + 2) + 4 * tm * d * 4 + 6 * tm * de2 * 4
    return pl.pallas_call(
        functools.partial(_expert_kernel, layer=layer),
        grid_spec=pltpu.PrefetchScalarGridSpec(
            num_scalar_prefetch=4,
            grid=(n_rows // tm,),
            in_specs=[
                pl.BlockSpec((tm, d), row_map),
                pl.BlockSpec(memory_space=pl.ANY),
                pl.BlockSpec(memory_space=pl.ANY),
            ],
            out_specs=pl.BlockSpec((tm, d), lambda i, *_: (i, 0)),
            scratch_shapes=[
                pltpu.VMEM((d, de2), F32), pltpu.VMEM((de, d), F32),
                pltpu.VMEM((d, de2), BF16), pltpu.VMEM((de, d), BF16),
                pltpu.SemaphoreType.DMA((2,)),
            ],
        ),
        out_shape=jax.ShapeDtypeStruct((n_rows, d), F32),
        compiler_params=pltpu.CompilerParams(
            dimension_semantics=("arbitrary",), vmem_limit_bytes=_vmem_limit(vmem)),
        name="moe_experts",
    )(tile_e, first, nxt, n_act, xs, w_gate_up, w_down)


def _combine_kernel(dest_ref, r_ref, route_ref, y_ref, *rest, tile_of, final):
    if final:
        g_ref, o_ref, ybuf, sem = rest
    else:
        o_ref, ybuf, sem = rest
    tm = r_ref.shape[0]
    tile = tile_of(pl.program_id(0))

    def row_copy(t, k):
        d = dest_ref[(tile * tm + t) * 2 + k]
        return pltpu.make_async_copy(y_ref.at[pl.ds(d, 1)], ybuf.at[k, pl.ds(t, 1)], sem)

    def issue(t, _):
        row_copy(t, 0).start()
        row_copy(t, 1).start()
        return 0

    def drain(t, _):
        row_copy(t, 0).wait()
        row_copy(t, 1).wait()
        return 0

    lax.fori_loop(0, tm, issue, 0)
    lax.fori_loop(0, tm, drain, 0)

    route = route_ref[...]
    out = r_ref[...] + route[:, 2:3] * ybuf[0] + route[:, 3:4] * ybuf[1]
    if final:
        ms = jnp.mean(out * out, axis=-1, keepdims=True)
        out = out * lax.rsqrt(ms + RMS_EPS) * g_ref[...]
    o_ref[...] = out


def _combine(dest, r, route, y, tiles_per_seq, final_gain=None):
    n_tok, d = r.shape
    tm = MOE_TM
    final = final_gain is not None
    if final:
        real_tiles = tiles_per_seq - P0 // tm
        tile_of = lambda i: (i // real_tiles) * tiles_per_seq + P0 // tm + i % real_tiles
        n_out_tiles = (n_tok // tm // tiles_per_seq) * real_tiles
    else:
        tile_of = lambda i: i
        n_out_tiles = n_tok // tm
    in_specs = [
        pl.BlockSpec((tm, d), lambda i, *_: (tile_of(i), 0)),
        pl.BlockSpec((tm, LANES), lambda i, *_: (tile_of(i), 0)),
        pl.BlockSpec(memory_space=pl.ANY),
    ]
    args = [dest, r, route, y]
    if final:
        in_specs.append(pl.BlockSpec((1, d), lambda i, *_: (0, 0)))
        args.append(final_gain)
    return pl.pallas_call(
        functools.partial(_combine_kernel, tile_of=tile_of, final=final),
        grid_spec=pltpu.PrefetchScalarGridSpec(
            num_scalar_prefetch=1,
            grid=(n_out_tiles,),
            in_specs=in_specs,
            out_specs=pl.BlockSpec((tm, d), lambda i, *_: (i, 0)),
            scratch_shapes=[pltpu.VMEM((2, tm, d), F32), pltpu.SemaphoreType.DMA],
        ),
        out_shape=jax.ShapeDtypeStruct((n_out_tiles * tm, d), F32),
        compiler_params=pltpu.CompilerParams(dimension_semantics=("arbitrary",)),
        name="moe_combine_final" if final else "moe_combine",
    )(*args)


def _routing_tables(route, counts, n_tiles_max, valid_tok):
    tm = MOE_TM
    cnt = counts[0, ROUTE_FIRST_EXPERT_LANE:ROUTE_FIRST_EXPERT_LANE + N_EXPERTS].astype(I32)
    ntile = (cnt + tm - 1) // tm
    tile_end = jnp.cumsum(ntile)
    tile_start = tile_end - ntile
    n_act = tile_end[-1:]
    e_ids = route[:, 0:2].astype(I32)
    rank = route[:, 4:6].astype(I32)
    dest = jnp.where(valid_tok[:, None], tile_start[e_ids] * tm + rank, 0).reshape(-1)
    all_tiles = jnp.arange(n_tiles_max, dtype=I32)
    clamped = jnp.minimum(all_tiles, n_act[0] - 1)
    tile_e = jnp.sum((clamped[:, None] >= tile_end[None, :]).astype(I32), axis=1)
    tile_e = jnp.minimum(tile_e, N_EXPERTS - 1)
    first = jnp.concatenate([jnp.ones((1,), I32), (tile_e[1:] != tile_e[:-1]).astype(I32)])
    end_of_mine = tile_end[tile_e]
    nxt = jnp.where(end_of_mine < n_act[0], tile_e[jnp.minimum(end_of_mine, n_tiles_max - 1)], -1)
    is_expert_tail = jnp.any((all_tiles[:, None] == tile_end[None, :] - 1) & (ntile[None, :] > 0), axis=1)
    zflag = (is_expert_tail | (all_tiles >= n_act[0])).astype(I32)
    return dest, tile_e, first, nxt.astype(I32), n_act.astype(I32), zflag


def kernel(x, meta, norm_mix, w_in, b_forget, w_attn_out, conv_w, conv_b, conv_ln_g, conv_ln_b,
           w_conv_out, w_out, norm_ffn, w_router_group, b_router_group, w_router_expert,
           b_router_expert, w_gate_up, w_down, norm_final):
    batch, seq, d = x.shape
    depth = w_in.shape[0]
    heads = b_forget.shape[1]
    att_w = heads * HEAD_DIM
    conv_c = conv_b.shape[1]
    lp = P0 + seq
    tiles_per_seq = lp // SEQ_TILE
    n_tok = batch * lp
    assert seq % SEQ_TILE == 0 and n_tok % INPROJ_TM == 0 and d % INPROJ_TN == 0
    assert att_w % LANES == 0 and conv_c % LANES == 0 and SEQ_TILE == MOE_TM

    front = jnp.concatenate([jnp.zeros((META0, d), x.dtype), meta.astype(x.dtype)], axis=0)
    r = jnp.concatenate([jnp.broadcast_to(front[None], (batch, P0, d)), x], axis=1).reshape(n_tok, d)

    c_q, c_k, c_v = 0, att_w, 2 * att_w
    c_f = 3 * att_w
    c_u = c_f + heads
    c_ga = c_u + 2 * conv_c
    c_gc = c_ga + d
    m_ga, m_gc = 0, d
    m_q = 2 * d
    m_k, m_v = m_q + att_w, m_q + 2 * att_w
    m_ua = m_q + 3 * att_w
    m_ub = m_ua + conv_c

    valid_tok = jnp.tile(jnp.arange(lp) >= META0, batch)
    n_pairs = 2 * batch * (lp - META0)
    n_tiles_max = n_pairs // MOE_TM + N_EXPERTS
    n_rows = n_tiles_max * MOE_TM

    out = None
    for l in range(depth):
        wl = w_in[l]
        w_main = jnp.concatenate(
            [wl[:, c_ga:c_ga + d], wl[:, c_gc:c_gc + d], wl[:, c_q:c_f], wl[:, c_u:c_ga]],
            axis=1).astype(BF16)
        wf = jnp.pad(wl[:, c_f:c_u], ((0, 0), (0, LANES - heads))).astype(BF16)
        z, f = _inproj(r, norm_mix[l][None], w_main, wf)
        cum = _forget_cumsum(f.reshape(batch, lp, LANES),
                             jnp.pad(b_forget[l], (0, LANES - heads))[None])
        z3 = z.reshape(batch, lp, -1)
        att = _attention(z3, cum, m_q, m_k, m_v).reshape(n_tok, att_w)
        conv_w_pad = jnp.concatenate([conv_w[l], jnp.zeros((1, conv_c), F32)], axis=0)
        cv = _conv_branch(z3, m_ua, m_ub, conv_w_pad, conv_b[l][None], conv_ln_g[l][None],
                          conv_ln_b[l][None]).reshape(n_tok, conv_c)

        wr = jnp.zeros((d, LANES), F32)
        wr = wr.at[:, :N_GROUPS].set(w_router_group[l])
        wr = wr.at[:, N_GROUPS:N_GROUPS + N_EXPERTS].set(w_router_expert[l]).astype(BF16)
        br = jnp.zeros((1, LANES), F32)
        br = br.at[0, :N_GROUPS].set(b_router_group[l])
        br = br.at[0, N_GROUPS:N_GROUPS + N_EXPERTS].set(b_router_expert[l])
        r, n2, route, counts = _merge_route(
            att, cv, z, m_ga, m_gc, r, w_attn_out[l].astype(BF16), w_conv_out[l].astype(BF16),
            w_out[l].astype(BF16), norm_ffn[l][None], wr, br, tiles_per_seq)

        dest, tile_e, first, nxt, n_act, zflag = _routing_tables(route, counts, n_tiles_max, valid_tok)
        xs = _dispatch(dest, zflag, n2, n_rows, tiles_per_seq)
        y = _experts(tile_e, first, nxt, n_act, xs, w_gate_up, w_down, l)
        if l + 1 < depth:
            r = _combine(dest, r, route, y, tiles_per_seq)
        else:
            out = _combine(dest, r, route, y, tiles_per_seq, final_gain=norm_final[None])
    return out.reshape(batch, seq, d)
```

```python
import functools

import jax
import jax.numpy as jnp
from jax import lax
from jax.experimental import pallas as pl
from jax.experimental.pallas import tpu as pltpu

F32 = jnp.float32
BF16 = jnp.bfloat16
I32 = jnp.int32
U32 = jnp.uint32

LANES = 128
SUBLANES = 8
MXU_DIM = 256
VMEM_BYTES_V7X = 64 * 1024 * 1024

N_META = 16
HEAD_DIM = 64
HEADS_PER_BLOCK = LANES // HEAD_DIM
ATT_HEADS = 4
ATT_BLOCK = ATT_HEADS * HEAD_DIM
CONV_K = 31
N_GROUPS = 4
EXPERTS_PER_GROUP = 8
N_EXPERTS = N_GROUPS * EXPERTS_PER_GROUP
LOG2_E = 1.4426950408889634
RMS_EPS = 1e-6
LN_EPS = 1e-5

SEQ_TILE = 256
P0 = SEQ_TILE
META0 = P0 - N_META
INPROJ_TM = 1024
INPROJ_TN = 512
MOE_TM = 256
CONV_HALO = 32
DMA_UNROLL = 8
ROUTE_FIRST_EXPERT_LANE = N_GROUPS
MASK_VALUE = -1e30


def _vmem_limit(nbytes):
    return int(min(max(nbytes, 16 * 1024 * 1024), VMEM_BYTES_V7X - 6 * 1024 * 1024))


def _sigmoid(x):
    return 1.0 / (1.0 + jnp.exp(-x))


def _pack_rows(v):
    c = v.shape[1] // 2
    lo = lax.bitcast_convert_type(v[:, :c].astype(BF16).astype(F32), U32)
    hi = lax.bitcast_convert_type(v[:, c:].astype(BF16).astype(F32), U32)
    return (hi & jnp.uint32(0xFFFF0000)) | (lo >> 16)


def _unpack_rows(w):
    lo = lax.bitcast_convert_type(w << 16, F32)
    hi = lax.bitcast_convert_type(w & jnp.uint32(0xFFFF0000), F32)
    return jnp.concatenate([lo, hi], axis=1)


def _inproj_kernel(r_ref, g_ref, w_ref, wf_ref, z_ref, f_ref, n_sc):
    @pl.when(pl.program_id(1) == 0)
    def _():
        x = r_ref[...]
        ms = jnp.mean(x * x, axis=-1, keepdims=True)
        n = (x * lax.rsqrt(ms + RMS_EPS) * g_ref[...]).astype(BF16)
        n_sc[...] = n
        f_ref[...] = jnp.dot(n, wf_ref[...], preferred_element_type=F32)

    z_ref[...] = jnp.dot(n_sc[...], w_ref[...], preferred_element_type=F32).astype(z_ref.dtype)


def _inproj(r, gain, w_main, wf):
    n_tok, d = r.shape
    cols = w_main.shape[1]
    tm, tn = INPROJ_TM, INPROJ_TN
    vmem = 2 * tm * d * 4 + 2 * d * tn * 2 + 2 * tm * tn * 2 + tm * d * 2 + 2 * (d * 2 + tm * 4) * LANES
    return pl.pallas_call(
        _inproj_kernel,
        grid=(n_tok // tm, cols // tn),
        in_specs=[
            pl.BlockSpec((tm, d), lambda i, j: (i, 0)),
            pl.BlockSpec((1, d), lambda i, j: (0, 0)),
            pl.BlockSpec((d, tn), lambda i, j: (0, j)),
            pl.BlockSpec((d, LANES), lambda i, j: (0, 0)),
        ],
        out_specs=[
            pl.BlockSpec((tm, tn), lambda i, j: (i, j)),
            pl.BlockSpec((tm, LANES), lambda i, j: (i, 0)),
        ],
        out_shape=[
            jax.ShapeDtypeStruct((n_tok, cols), BF16),
            jax.ShapeDtypeStruct((n_tok, LANES), F32),
        ],
        scratch_shapes=[pltpu.VMEM((tm, d), BF16)],
        compiler_params=pltpu.CompilerParams(
            dimension_semantics=("parallel", "arbitrary"),
            vmem_limit_bytes=_vmem_limit(vmem + (8 << 20))),
        name="inproj",
    )(r, gain, w_main, wf)


def _cum_kernel(f_ref, b_ref, o_ref):
    ch = MXU_DIM
    lp = f_ref.shape[1]
    row = lax.broadcasted_iota(I32, (ch, ch), 0)
    col = lax.broadcasted_iota(I32, (ch, ch), 1)
    lower = (col <= row).astype(F32)
    carry = jnp.zeros((1, LANES), F32)
    for c in range(lp // ch):
        x = f_ref[0, c * ch:(c + 1) * ch, :] + b_ref[...]
        log_f = jnp.minimum(x, 0.0) - jnp.log1p(jnp.exp(-jnp.abs(x)))
        loc = jnp.dot(lower, log_f, preferred_element_type=F32,
                      precision=lax.Precision.HIGHEST) + carry
        o_ref[0, c * ch:(c + 1) * ch, :] = loc
        carry = loc[ch - 1:ch, :]


def _forget_cumsum(f3, b_forget):
    batch, lp, _ = f3.shape
    return pl.pallas_call(
        _cum_kernel,
        grid=(batch,),
        in_specs=[
            pl.BlockSpec((1, lp, LANES), lambda b: (b, 0, 0)),
            pl.BlockSpec((1, LANES), lambda b: (0, 0)),
        ],
        out_specs=pl.BlockSpec((1, lp, LANES), lambda b: (b, 0, 0)),
        out_shape=jax.ShapeDtypeStruct((batch, lp, LANES), F32),
        compiler_params=pltpu.CompilerParams(dimension_semantics=("parallel",)),
        name="forget_cumsum",
    )(f3, b_forget)


def _attn_kernel(q_ref, k_ref, v_ref, c_ref, o_ref, kaug_sc, vt_sc, qaug_sc, s_a, s_b,
                 m_sc, l_sc, acc_sc):
    tq = q_ref.shape[1]
    tk = tq
    lp = k_ref.shape[1]
    hg = pl.program_id(1)
    qi = pl.program_id(2)
    lane = lax.broadcasted_iota(I32, (1, LANES), 1)
    nt_dims = (((1,), (1,)), ((), ()))

    def own_lanes(h):
        return lane < HEAD_DIM if h % HEADS_PER_BLOCK == 0 else lane >= HEAD_DIM

    def bias_lane(h):
        return HEAD_DIM if h % HEADS_PER_BLOCK == 0 else 0

    def block_lanes(h):
        blk = h // HEADS_PER_BLOCK
        return slice(blk * LANES, (blk + 1) * LANES)

    @pl.when(qi == 0)
    def _():
        for c in range(lp // tk):
            rows = slice(c * tk, (c + 1) * tk)
            vt_sc[:, rows] = v_ref[0, rows, :].astype(F32).T.astype(BF16)
            cum = c_ref[0, rows, :]
            key_pos = c * tk + lax.broadcasted_iota(I32, (tk, 1), 0)
            for h in range(ATT_HEADS):
                head = hg * ATT_HEADS + h
                col = jnp.sum(jnp.where(lane == head, cum, 0.0), axis=-1, keepdims=True) * LOG2_E
                hi = col.astype(BF16).astype(F32)
                rem = col - hi
                mid = rem.astype(BF16).astype(F32)
                low = rem - mid
                hi = jnp.where(key_pos < META0, -MASK_VALUE, hi)
                a = bias_lane(h)
                bias = jnp.where(lane == a, hi, jnp.where(lane == a + 1, mid,
                                                          jnp.where(lane == a + 2, low, 0.0)))
                kaug_sc[h, rows, :] = jnp.where(own_lanes(h), k_ref[0, rows, block_lanes(h)],
                                                bias.astype(BF16))

    for h in range(ATT_HEADS):
        a = bias_lane(h)
        minus_one = jnp.where((lane >= a) & (lane < a + 3), -1.0, 0.0)
        scaled = q_ref[0, :, block_lanes(h)].astype(F32) * (HEAD_DIM ** -0.5 * LOG2_E)
        qaug_sc[h] = jnp.where(own_lanes(h), scaled, minus_one).astype(BF16)
        m_sc[h] = jnp.full((1, tq), MASK_VALUE, F32)
        l_sc[h] = jnp.zeros((1, tq), F32)
        acc_sc[h] = jnp.zeros((HEAD_DIM, tq), F32)

    def scores_into(dst, j):
        s0 = pl.multiple_of(j * tk, tk)
        for h in range(ATT_HEADS):
            dst[h] = lax.dot_general(kaug_sc[h, pl.ds(s0, tk), :], qaug_sc[h], nt_dims,
                                     preferred_element_type=F32)

    def softmax_pv(src, j, diagonal):
        s0 = pl.multiple_of(j * tk, tk)
        for h in range(ATT_HEADS):
            st = src[h]
            if diagonal:
                key = lax.broadcasted_iota(I32, (tk, tq), 0)
                qry = lax.broadcasted_iota(I32, (tk, tq), 1)
                st = jnp.where(key <= qry, st, MASK_VALUE)
            m = m_sc[h]
            m_new = jnp.maximum(m, jnp.max(st, axis=0, keepdims=True))
            alpha = jnp.exp2(m - m_new)
            p = jnp.exp2(st - m_new)
            l_sc[h] = alpha * l_sc[h] + jnp.sum(p, axis=0, keepdims=True)
            pv = jnp.dot(vt_sc[h * HEAD_DIM:(h + 1) * HEAD_DIM, pl.ds(s0, tk)], p.astype(BF16),
                         preferred_element_type=F32)
            acc_sc[h] = alpha * acc_sc[h] + pv
            m_sc[h] = m_new

    scores_into(s_a, 0)

    def pair(i, _):
        j = 2 * i
        scores_into(s_b, j + 1)
        softmax_pv(s_a, j, False)
        scores_into(s_a, j + 2)
        softmax_pv(s_b, j + 1, False)
        return 0

    lax.fori_loop(0, qi // 2, pair, 0)

    @pl.when(qi % 2 == 0)
    def _():
        softmax_pv(s_a, qi, True)

    @pl.when(qi % 2 == 1)
    def _():
        scores_into(s_b, qi)
        softmax_pv(s_a, qi - 1, False)
        softmax_pv(s_b, qi, True)

    att_t = jnp.concatenate([acc_sc[h] / l_sc[h] for h in range(ATT_HEADS)], axis=0)
    o_ref[0] = att_t.T.astype(o_ref.dtype)


def _attention(z3, cum, col_q, col_k, col_v):
    batch, lp, _ = z3.shape
    att_w = col_k - col_q
    tq = SEQ_TILE
    qb, kb, vb = col_q // ATT_BLOCK, col_k // ATT_BLOCK, col_v // ATT_BLOCK
    return pl.pallas_call(
        _attn_kernel,
        grid=(batch, att_w // ATT_BLOCK, lp // tq),
        in_specs=[
            pl.BlockSpec((1, tq, ATT_BLOCK), lambda b, h, i: (b, i, qb + h)),
            pl.BlockSpec((1, lp, ATT_BLOCK), lambda b, h, i: (b, 0, kb + h)),
            pl.BlockSpec((1, lp, ATT_BLOCK), lambda b, h, i: (b, 0, vb + h)),
            pl.BlockSpec((1, lp, LANES), lambda b, h, i: (b, 0, 0)),
        ],
        out_specs=pl.BlockSpec((1, tq, ATT_BLOCK), lambda b, h, i: (b, i, h)),
        out_shape=jax.ShapeDtypeStruct((batch, lp, att_w), BF16),
        scratch_shapes=[
            pltpu.VMEM((ATT_HEADS, lp, LANES), BF16), pltpu.VMEM((ATT_BLOCK, lp), BF16),
            pltpu.VMEM((ATT_HEADS, tq, LANES), BF16),
            pltpu.VMEM((ATT_HEADS, tq, tq), F32), pltpu.VMEM((ATT_HEADS, tq, tq), F32),
            pltpu.VMEM((ATT_HEADS, 1, tq), F32), pltpu.VMEM((ATT_HEADS, 1, tq), F32),
            pltpu.VMEM((ATT_HEADS, HEAD_DIM, tq), F32),
        ],
        compiler_params=pltpu.CompilerParams(
            dimension_semantics=("parallel", "parallel", "arbitrary")),
        name="fox_attention",
    )(z3, z3, z3, cum)


def _conv_kernel(a_ref, b_ref, w_ref, cb_ref, g_ref, lb_ref, o_ref, zbuf, ybuf, zshift):
    tt = a_ref.shape[1]
    ch = a_ref.shape[2]
    t = pl.program_id(1)

    @pl.when(t == 0)
    def _():
        zbuf[0:CONV_HALO, :] = jnp.zeros((CONV_HALO, ch), F32)

    z = a_ref[0].astype(F32) * _sigmoid(b_ref[0].astype(F32))
    rows = t * tt + lax.broadcasted_iota(I32, (tt, 1), 0)
    zbuf[CONV_HALO:CONV_HALO + tt, :] = jnp.where(rows >= META0, z, 0.0)

    first = CONV_HALO - (CONV_K - 1)

    def chan_block(cb, _):
        c0 = pl.multiple_of(cb * LANES, LANES)
        for res in range(SUBLANES):
            span = tt + ((CONV_K - 1 - res) // SUBLANES) * SUBLANES
            zshift[res, 0:span, :] = zbuf[first + res:first + res + span, pl.ds(c0, LANES)]
        acc = jnp.zeros((tt, LANES), F32) + cb_ref[:, pl.ds(c0, LANES)]
        for k in range(CONV_K):
            base = (k // SUBLANES) * SUBLANES
            acc = acc + w_ref[k:k + 1, pl.ds(c0, LANES)] * zshift[k % SUBLANES, base:base + tt, :]
        ybuf[:, pl.ds(c0, LANES)] = acc
        return 0

    lax.fori_loop(0, ch // LANES, chan_block, 0)

    y = ybuf[...]
    mu = jnp.mean(y, axis=-1, keepdims=True)
    yc = y - mu
    var = jnp.mean(yc * yc, axis=-1, keepdims=True)
    zn = yc * lax.rsqrt(var + LN_EPS) * g_ref[...] + lb_ref[...]
    o_ref[0] = (zn * _sigmoid(zn)).astype(o_ref.dtype)
    zbuf[0:CONV_HALO, :] = zbuf[tt:tt + CONV_HALO, :]


def _conv_branch(z3, col_a, col_b, conv_w, conv_b, ln_g, ln_b):
    batch, lp, _ = z3.shape
    ch = conv_b.shape[1]
    tt = SEQ_TILE
    ab, bb = col_a // ch, col_b // ch
    return pl.pallas_call(
        _conv_kernel,
        grid=(batch, lp // tt),
        in_specs=[
            pl.BlockSpec((1, tt, ch), lambda b, t: (b, t, ab)),
            pl.BlockSpec((1, tt, ch), lambda b, t: (b, t, bb)),
            pl.BlockSpec(conv_w.shape, lambda b, t: (0, 0)),
            pl.BlockSpec((1, ch), lambda b, t: (0, 0)),
            pl.BlockSpec((1, ch), lambda b, t: (0, 0)),
            pl.BlockSpec((1, ch), lambda b, t: (0, 0)),
        ],
        out_specs=pl.BlockSpec((1, tt, ch), lambda b, t: (b, t, 0)),
        out_shape=jax.ShapeDtypeStruct((batch, lp, ch), BF16),
        scratch_shapes=[
            pltpu.VMEM((tt + CONV_HALO, ch), F32), pltpu.VMEM((tt, ch), F32),
            pltpu.VMEM((SUBLANES, tt + ((CONV_K - 1) // SUBLANES) * SUBLANES, LANES), F32),
        ],
        compiler_params=pltpu.CompilerParams(dimension_semantics=("parallel", "arbitrary")),
        name="conv_branch",
    )(z3, z3, conv_w, conv_b, ln_g, ln_b)


def _merge_route_kernel(att_ref, cv_ref, ga_ref, gc_ref, r_ref, wao_ref, wco_ref, wo_ref,
                        g_ref, wr_ref, br_ref, ro_ref, n_ref, route_ref, cnt_ref, cnt_sc,
                        *, tiles_per_seq):
    i = pl.program_id(0)
    tm = r_ref.shape[0]

    @pl.when(i == 0)
    def _():
        cnt_sc[...] = jnp.zeros_like(cnt_sc)

    br_att = jnp.dot(att_ref[...], wao_ref[...], preferred_element_type=F32)
    br_conv = jnp.dot(cv_ref[...], wco_ref[...], preferred_element_type=F32)
    merged = _sigmoid(ga_ref[...].astype(F32)) * br_att + _sigmoid(gc_ref[...].astype(F32)) * br_conv
    r_new = r_ref[...] + jnp.dot(merged.astype(BF16), wo_ref[...], preferred_element_type=F32)
    ro_ref[...] = r_new

    ms = jnp.mean(r_new * r_new, axis=-1, keepdims=True)
    n = r_new * lax.rsqrt(ms + RMS_EPS) * g_ref[...]
    n_ref[...] = _pack_rows(n)

    logits = jnp.dot(n.astype(BF16), wr_ref[...], preferred_element_type=F32) + br_ref[...]
    lane = lax.broadcasted_iota(I32, logits.shape, 1)
    big = jnp.int32(4 * LANES)
    first_e = ROUTE_FIRST_EXPERT_LANE

    gl = jnp.where(lane < N_GROUPS, logits, -jnp.inf)
    gmax = jnp.max(gl, axis=-1, keepdims=True)
    gsum = jnp.sum(jnp.exp(gl - gmax), axis=-1, keepdims=True)
    g_w = 1.0 / gsum
    g_idx = jnp.min(jnp.where(gl == gmax, lane, big), axis=-1, keepdims=True)

    in_group = (lane >= first_e + g_idx * EXPERTS_PER_GROUP) & \
               (lane < first_e + (g_idx + 1) * EXPERTS_PER_GROUP)
    el = jnp.where(in_group, logits, -jnp.inf)
    v0 = jnp.max(el, axis=-1, keepdims=True)
    i0 = jnp.min(jnp.where(el == v0, lane, big), axis=-1, keepdims=True)
    el = jnp.where(lane == i0, -jnp.inf, el)
    v1 = jnp.max(el, axis=-1, keepdims=True)
    i1 = jnp.min(jnp.where(el == v1, lane, big), axis=-1, keepdims=True)
    e1 = jnp.exp(v1 - v0)
    w0 = g_w / (1.0 + e1)
    w1 = g_w * e1 / (1.0 + e1)

    pos = (i % tiles_per_seq) * tm + lax.broadcasted_iota(I32, (tm, 1), 0)
    valid = pos >= META0
    onehot = jnp.where(((lane == i0) | (lane == i1)) & valid, 1.0, 0.0)
    rr = lax.broadcasted_iota(I32, (tm, tm), 0)
    cc = lax.broadcasted_iota(I32, (tm, tm), 1)
    lower = jnp.where(cc < rr, 1.0, 0.0).astype(BF16)
    rank = jnp.dot(lower, onehot.astype(BF16), preferred_element_type=F32) + cnt_sc[0:1, :]
    rank0 = jnp.sum(jnp.where(lane == i0, rank, 0.0), axis=-1, keepdims=True)
    rank1 = jnp.sum(jnp.where(lane == i1, rank, 0.0), axis=-1, keepdims=True)
    cnt_sc[0:1, :] = cnt_sc[0:1, :] + jnp.sum(onehot, axis=0, keepdims=True)
    cnt_ref[...] = jnp.broadcast_to(cnt_sc[0:1, :], cnt_ref.shape)

    w0 = jnp.where(valid, w0, 0.0)
    w1 = jnp.where(valid, w1, 0.0)
    out = jnp.where(lane == 0, (i0 - first_e).astype(F32), 0.0)
    out = jnp.where(lane == 1, (i1 - first_e).astype(F32), out)
    out = jnp.where(lane == 2, w0, out)
    out = jnp.where(lane == 3, w1, out)
    out = jnp.where(lane == 4, rank0, out)
    out = jnp.where(lane == 5, rank1, out)
    route_ref[...] = out


def _merge_route(att, cv, z, col_ga, col_gc, r, wao, wco, wo, gain, wr, br, tiles_per_seq):
    n_tok, d = r.shape
    aw = att.shape[1]
    cw = cv.shape[1]
    tm = SEQ_TILE
    gab, gcb = col_ga // d, col_gc // d
    const = lambda i: (0, 0)
    single = pl.Buffered(1)
    vmem = (2 * tm * (aw * 2 + cw * 2 + 2 * d * 2 + 3 * d * 4 + LANES * 4)
            + (aw * d + cw * d + d * d + d * LANES) * 2 + 8 * tm * d * 4)
    return pl.pallas_call(
        functools.partial(_merge_route_kernel, tiles_per_seq=tiles_per_seq),
        grid=(n_tok // tm,),
        in_specs=[
            pl.BlockSpec((tm, aw), lambda i: (i, 0)),
            pl.BlockSpec((tm, cw), lambda i: (i, 0)),
            pl.BlockSpec((tm, d), lambda i: (i, gab)),
            pl.BlockSpec((tm, d), lambda i: (i, gcb)),
            pl.BlockSpec((tm, d), lambda i: (i, 0)),
            pl.BlockSpec(wao.shape, const, pipeline_mode=single),
            pl.BlockSpec(wco.shape, const, pipeline_mode=single),
            pl.BlockSpec(wo.shape, const, pipeline_mode=single),
            pl.BlockSpec((1, d), const),
            pl.BlockSpec(wr.shape, const, pipeline_mode=single),
            pl.BlockSpec((1, LANES), const),
        ],
        out_specs=[
            pl.BlockSpec((tm, d), lambda i: (i, 0)),
            pl.BlockSpec((tm, d // 2), lambda i: (i, 0)),
            pl.BlockSpec((tm, LANES), lambda i: (i, 0)),
            pl.BlockSpec((8, LANES), const),
        ],
        out_shape=[
            jax.ShapeDtypeStruct((n_tok, d), F32),
            jax.ShapeDtypeStruct((n_tok, d // 2), U32),
            jax.ShapeDtypeStruct((n_tok, LANES), F32),
            jax.ShapeDtypeStruct((8, LANES), F32),
        ],
        scratch_shapes=[pltpu.VMEM((8, LANES), F32)],
        compiler_params=pltpu.CompilerParams(
            dimension_semantics=("arbitrary",), vmem_limit_bytes=_vmem_limit(vmem)),
        name="merge_route",
    )(att, cv, z, z, r, wao, wco, wo, gain, wr, br)


def _dispatch_kernel(dest_ref, zflag_ref, n_ref, xs_ref, zero_sc, stage, sem, zsem, *, tiles_per_seq):
    i = pl.program_id(0)
    tm = n_ref.shape[0]
    n_tiles = xs_ref.shape[0] // tm

    def zero_copy(t):
        return pltpu.make_async_copy(zero_sc, xs_ref.at[pl.ds(pl.multiple_of(t * tm, tm), tm)], zsem)

    @pl.when(i == 0)
    def _():
        zero_sc[...] = jnp.zeros_like(zero_sc)

        def issue_zero(t, _):
            @pl.when(zflag_ref[t] != 0)
            def _():
                zero_copy(t).start()
            return 0

        def drain_zero(t, _):
            @pl.when(zflag_ref[t] != 0)
            def _():
                zero_copy(t).wait()
            return 0

        lax.fori_loop(0, n_tiles, issue_zero, 0)
        lax.fori_loop(0, n_tiles, drain_zero, 0)

    slot = i % 2
    last = pl.num_programs(0) - 1

    def first_row_group(step):
        return jnp.where(step % tiles_per_seq == 0, META0, 0) // DMA_UNROLL

    def row_copy(step, buf, t, k):
        d = dest_ref[(step * tm + t) * 2 + k]
        return pltpu.make_async_copy(stage.at[buf, pl.ds(t, 1)], xs_ref.at[pl.ds(d, 1)], sem.at[buf])

    def drain(step, buf):
        one_row = pltpu.make_async_copy(stage.at[buf, pl.ds(0, 1)], xs_ref.at[pl.ds(0, 1)], sem.at[buf])

        def body(g, _):
            for _u in range(2 * DMA_UNROLL):
                one_row.wait()
            return 0
        lax.fori_loop(first_row_group(step), tm // DMA_UNROLL, body, 0)

    @pl.when(i >= 2)
    def _():
        drain(i - 2, slot)

    stage[slot] = n_ref[...]

    def issue(g, _):
        for u in range(DMA_UNROLL):
            row_copy(i, slot, g * DMA_UNROLL + u, 0).start()
            row_copy(i, slot, g * DMA_UNROLL + u, 1).start()
        return 0

    lax.fori_loop(first_row_group(i), tm // DMA_UNROLL, issue, 0)

    @pl.when(i == last)
    def _():
        @pl.when(i >= 1)
        def _():
            drain(i - 1, 1 - slot)
        drain(i, slot)


def _dispatch(dest, zflag, n, n_rows, tiles_per_seq):
    n_tok, w = n.shape
    tm = MOE_TM
    return pl.pallas_call(
        functools.partial(_dispatch_kernel, tiles_per_seq=tiles_per_seq),
        grid_spec=pltpu.PrefetchScalarGridSpec(
            num_scalar_prefetch=2,
            grid=(n_tok // tm,),
            in_specs=[pl.BlockSpec((tm, w), lambda i, *_: (i, 0))],
            out_specs=pl.BlockSpec(memory_space=pl.ANY),
            scratch_shapes=[pltpu.VMEM((tm, w), n.dtype), pltpu.VMEM((2, tm, w), n.dtype),
                            pltpu.SemaphoreType.DMA((2,)), pltpu.SemaphoreType.DMA],
        ),
        out_shape=jax.ShapeDtypeStruct((n_rows, w), n.dtype),
        compiler_params=pltpu.CompilerParams(dimension_semantics=("arbitrary",),
                                             has_side_effects=True),
        name="moe_dispatch",
    )(dest, zflag, n)


def _expert_kernel(te_ref, first_ref, nxt_ref, na_ref, x_ref, wgu_hbm, wdn_hbm, y_ref,
                   gu_stage, dn_stage, gu_bf, dn_bf, sem, *, layer):
    i = pl.program_id(0)
    active = i < na_ref[0]

    def weight_copies(e):
        return (pltpu.make_async_copy(wgu_hbm.at[layer, e], gu_stage, sem.at[0]),
                pltpu.make_async_copy(wdn_hbm.at[layer, e], dn_stage, sem.at[1]))

    @pl.when(i == 0)
    def _():
        for cp in weight_copies(te_ref[0]):
            cp.start()

    @pl.when(active & (first_ref[i] != 0))
    def _():
        for cp in weight_copies(te_ref[i]):
            cp.wait()
        gu_bf[...] = gu_stage[...].astype(BF16)
        dn_bf[...] = dn_stage[...].astype(BF16)

        @pl.when(nxt_ref[i] >= 0)
        def _():
            for cp in weight_copies(nxt_ref[i]):
                cp.start()

    @pl.when(active)
    def _():
        de = dn_bf.shape[0]
        x = _unpack_rows(x_ref[...]).astype(BF16)
        h = jnp.dot(x, gu_bf[...], preferred_element_type=F32)
        a = h[:, :de]
        b = h[:, de:]
        act = a * _sigmoid(a) * b
        y_ref[...] = _pack_rows(jnp.dot(act.astype(BF16), dn_bf[...], preferred_element_type=F32))

    @pl.when(jnp.logical_not(active))
    def _():
        y_ref[...] = jnp.zeros_like(y_ref)


def _experts(tile_e, first, nxt, n_act, xs, w_gate_up, w_down, layer):
    n_rows, w = xs.shape
    d = w_gate_up.shape[2]
    assert d == 2 * w
    tm = MOE_TM
    de2 = w_gate_up.shape[3]
    de = w_down.shape[2]
    row_map = lambda i, te, fi, nx, na: (jnp.minimum(i, na[0] - 1), 0)
    vmem = (d * de2 + de * d) * (4 + 2) + 6 * tm * d * 4 + 6 * tm * de2 * 4
    return pl.pallas_call(
        functools.partial(_expert_kernel, layer=layer),
        grid_spec=pltpu.PrefetchScalarGridSpec(
            num_scalar_prefetch=4,
            grid=(n_rows // tm,),
            in_specs=[
                pl.BlockSpec((tm, w), row_map),
                pl.BlockSpec(memory_space=pl.ANY),
                pl.BlockSpec(memory_space=pl.ANY),
            ],
            out_specs=pl.BlockSpec((tm, w), lambda i, *_: (i, 0)),
            scratch_shapes=[
                pltpu.VMEM((d, de2), F32), pltpu.VMEM((de, d), F32),
                pltpu.VMEM((d, de2), BF16), pltpu.VMEM((de, d), BF16),
                pltpu.SemaphoreType.DMA((2,)),
            ],
        ),
        out_shape=jax.ShapeDtypeStruct((n_rows, w), xs.dtype),
        compiler_params=pltpu.CompilerParams(
            dimension_semantics=("arbitrary",), vmem_limit_bytes=_vmem_limit(vmem)),
        name="moe_experts",
    )(tile_e, first, nxt, n_act, xs, w_gate_up, w_down)


def _combine_kernel(dest_ref, r_ref, route_ref, y_ref, *rest, tile_of, final):
    if final:
        g_ref, o_ref, ybuf, sem = rest
    else:
        o_ref, ybuf, sem = rest
    tm = r_ref.shape[0]
    i = pl.program_id(0)
    slot = i % 2

    def issue(step, buf):
        tile = tile_of(step)

        def body(t, _):
            for k in range(2):
                d = dest_ref[(tile * tm + t) * 2 + k]
                pltpu.make_async_copy(y_ref.at[pl.ds(d, 1)], ybuf.at[buf, k, pl.ds(t, 1)],
                                      sem.at[buf]).start()
            return 0
        lax.fori_loop(0, tm, body, 0, unroll=DMA_UNROLL)

    @pl.when(i == 0)
    def _():
        issue(i, slot)

    @pl.when(i + 1 < pl.num_programs(0))
    def _():
        issue(i + 1, 1 - slot)

    one_row = pltpu.make_async_copy(y_ref.at[pl.ds(0, 1)], ybuf.at[slot, 0, pl.ds(0, 1)], sem.at[slot])

    def drain(t, _):
        one_row.wait()
        one_row.wait()
        return 0
    lax.fori_loop(0, tm, drain, 0, unroll=DMA_UNROLL)

    route = route_ref[...]
    out = (r_ref[...] + route[:, 2:3] * _unpack_rows(ybuf[slot, 0])
           + route[:, 3:4] * _unpack_rows(ybuf[slot, 1]))
    if final:
        ms = jnp.mean(out * out, axis=-1, keepdims=True)
        out = out * lax.rsqrt(ms + RMS_EPS) * g_ref[...]
    o_ref[...] = out


def _combine(dest, r, route, y, tiles_per_seq, final_gain=None):
    n_tok, d = r.shape
    tm = MOE_TM
    final = final_gain is not None
    if final:
        real_tiles = tiles_per_seq - P0 // tm
        tile_of = lambda i: (i // real_tiles) * tiles_per_seq + P0 // tm + i % real_tiles
        n_out_tiles = (n_tok // tm // tiles_per_seq) * real_tiles
    else:
        tile_of = lambda i: i
        n_out_tiles = n_tok // tm
    in_specs = [
        pl.BlockSpec((tm, d), lambda i, *_: (tile_of(i), 0)),
        pl.BlockSpec((tm, LANES), lambda i, *_: (tile_of(i), 0)),
        pl.BlockSpec(memory_space=pl.ANY),
    ]
    args = [dest, r, route, y]
    if final:
        in_specs.append(pl.BlockSpec((1, d), lambda i, *_: (0, 0)))
        args.append(final_gain)
    return pl.pallas_call(
        functools.partial(_combine_kernel, tile_of=tile_of, final=final),
        grid_spec=pltpu.PrefetchScalarGridSpec(
            num_scalar_prefetch=1,
            grid=(n_out_tiles,),
            in_specs=in_specs,
            out_specs=pl.BlockSpec((tm, d), lambda i, *_: (i, 0)),
            scratch_shapes=[pltpu.VMEM((2, 2, tm, d // 2), y.dtype), pltpu.SemaphoreType.DMA((2,))],
        ),
        out_shape=jax.ShapeDtypeStruct((n_out_tiles * tm, d), F32),
        compiler_params=pltpu.CompilerParams(
            dimension_semantics=("arbitrary",),
            vmem_limit_bytes=_vmem_limit(8 * tm * d * 4 + 4 * tm * d * 4 + (8 << 20))),
        name="moe_combine_final" if final else "moe_combine",
    )(*args)


def _routing_tables(route, counts, n_tiles_max, valid_tok):
    tm = MOE_TM
    cnt = counts[0, ROUTE_FIRST_EXPERT_LANE:ROUTE_FIRST_EXPERT_LANE + N_EXPERTS].astype(I32)
    ntile = (cnt + tm - 1) // tm
    tile_end = jnp.cumsum(ntile)
    tile_start = tile_end - ntile
    n_act = tile_end[-1:]
    e_ids = route[:, 0:2].astype(I32)
    rank = route[:, 4:6].astype(I32)
    dest = jnp.where(valid_tok[:, None], tile_start[e_ids] * tm + rank, 0).reshape(-1)
    all_tiles = jnp.arange(n_tiles_max, dtype=I32)
    clamped = jnp.minimum(all_tiles, n_act[0] - 1)
    tile_e = jnp.sum((clamped[:, None] >= tile_end[None, :]).astype(I32), axis=1)
    tile_e = jnp.minimum(tile_e, N_EXPERTS - 1)
    first = jnp.concatenate([jnp.ones((1,), I32), (tile_e[1:] != tile_e[:-1]).astype(I32)])
    end_of_mine = tile_end[tile_e]
    nxt = jnp.where(end_of_mine < n_act[0], tile_e[jnp.minimum(end_of_mine, n_tiles_max - 1)], -1)
    is_expert_tail = jnp.any((all_tiles[:, None] == tile_end[None, :] - 1) & (ntile[None, :] > 0), axis=1)
    zflag = (is_expert_tail | (all_tiles >= n_act[0])).astype(I32)
    return dest, tile_e, first, nxt.astype(I32), n_act.astype(I32), zflag


def kernel(x, meta, norm_mix, w_in, b_forget, w_attn_out, conv_w, conv_b, conv_ln_g, conv_ln_b,
           w_conv_out, w_out, norm_ffn, w_router_group, b_router_group, w_router_expert,
           b_router_expert, w_gate_up, w_down, norm_final):
    batch, seq, d = x.shape
    depth = w_in.shape[0]
    heads = b_forget.shape[1]
    att_w = heads * HEAD_DIM
    conv_c = conv_b.shape[1]
    lp = P0 + seq
    tiles_per_seq = lp // SEQ_TILE
    n_tok = batch * lp
    assert seq % SEQ_TILE == 0 and n_tok % INPROJ_TM == 0 and d % INPROJ_TN == 0
    assert att_w % LANES == 0 and conv_c % LANES == 0 and SEQ_TILE == MOE_TM

    front = jnp.concatenate([jnp.zeros((META0, d), x.dtype), meta.astype(x.dtype)], axis=0)
    r = jnp.concatenate([jnp.broadcast_to(front[None], (batch, P0, d)), x], axis=1).reshape(n_tok, d)

    c_q, c_k, c_v = 0, att_w, 2 * att_w
    c_f = 3 * att_w
    c_u = c_f + heads
    c_ga = c_u + 2 * conv_c
    c_gc = c_ga + d
    m_ga, m_gc = 0, d
    m_q = 2 * d
    m_k, m_v = m_q + att_w, m_q + 2 * att_w
    m_ua = m_q + 3 * att_w
    m_ub = m_ua + conv_c

    valid_tok = jnp.tile(jnp.arange(lp) >= META0, batch)
    n_pairs = 2 * batch * (lp - META0)
    n_tiles_max = n_pairs // MOE_TM + N_EXPERTS
    n_rows = n_tiles_max * MOE_TM

    out = None
    for l in range(depth):
        wl = w_in[l]
        w_main = jnp.concatenate(
            [wl[:, c_ga:c_ga + d], wl[:, c_gc:c_gc + d], wl[:, c_q:c_f], wl[:, c_u:c_ga]],
            axis=1).astype(BF16)
        wf = jnp.pad(wl[:, c_f:c_u], ((0, 0), (0, LANES - heads))).astype(BF16)
        z, f = _inproj(r, norm_mix[l][None], w_main, wf)
        cum = _forget_cumsum(f.reshape(batch, lp, LANES),
                             jnp.pad(b_forget[l], (0, LANES - heads))[None])
        z3 = z.reshape(batch, lp, -1)
        att = _attention(z3, cum, m_q, m_k, m_v).reshape(n_tok, att_w)
        conv_w_pad = jnp.concatenate([conv_w[l], jnp.zeros((1, conv_c), F32)], axis=0)
        cv = _conv_branch(z3, m_ua, m_ub, conv_w_pad, conv_b[l][None], conv_ln_g[l][None],
                          conv_ln_b[l][None]).reshape(n_tok, conv_c)

        wr = jnp.zeros((d, LANES), F32)
        wr = wr.at[:, :N_GROUPS].set(w_router_group[l])
        wr = wr.at[:, N_GROUPS:N_GROUPS + N_EXPERTS].set(w_router_expert[l]).astype(BF16)
        br = jnp.zeros((1, LANES), F32)
        br = br.at[0, :N_GROUPS].set(b_router_group[l])
        br = br.at[0, N_GROUPS:N_GROUPS + N_EXPERTS].set(b_router_expert[l])
        r, n2, route, counts = _merge_route(
            att, cv, z, m_ga, m_gc, r, w_attn_out[l].astype(BF16), w_conv_out[l].astype(BF16),
            w_out[l].astype(BF16), norm_ffn[l][None], wr, br, tiles_per_seq)

        dest, tile_e, first, nxt, n_act, zflag = _routing_tables(route, counts, n_tiles_max, valid_tok)
        xs = _dispatch(dest, zflag, n2, n_rows, tiles_per_seq)
        y = _experts(tile_e, first, nxt, n_act, xs, w_gate_up, w_down, l)
        if l + 1 < depth:
            r = _combine(dest, r, route, y, tiles_per_seq)
        else:
            out = _combine(dest, r, route, y, tiles_per_seq, final_gain=norm_final[None])
    return out.reshape(batch, seq, d)
```

```python
import functools

import jax
import jax.numpy as jnp
from jax import lax
from jax.experimental import pallas as pl
from jax.experimental.pallas import tpu as pltpu

F32 = jnp.float32
BF16 = jnp.bfloat16
I32 = jnp.int32
U32 = jnp.uint32

LANES = 128
SUBLANES = 8
MXU_DIM = 256
VMEM_BYTES_V7X = 64 * 1024 * 1024

N_META = 16
HEAD_DIM = 64
HEADS_PER_BLOCK = LANES // HEAD_DIM
ATT_HEADS = 8
ATT_BLOCK = ATT_HEADS * HEAD_DIM
CONV_K = 31
N_GROUPS = 4
EXPERTS_PER_GROUP = 8
N_EXPERTS = N_GROUPS * EXPERTS_PER_GROUP
LOG2_E = 1.4426950408889634
RMS_EPS = 1e-6
LN_EPS = 1e-5

SEQ_TILE = 256
P0 = SEQ_TILE
META0 = P0 - N_META
INPROJ_TM = 1024
INPROJ_TN = 1024
MOE_TM = 256
CONV_HALO = 32
DMA_UNROLL = 8
ROUTE_FIRST_EXPERT_LANE = N_GROUPS
MASK_VALUE = -1e30


def _vmem_limit(nbytes):
    return int(min(max(nbytes, 16 * 1024 * 1024), VMEM_BYTES_V7X - 6 * 1024 * 1024))


def _sigmoid(x):
    return 1.0 / (1.0 + jnp.exp(-x))


def _pack_rows(v):
    c = v.shape[1] // 2
    lo = lax.bitcast_convert_type(v[:, :c].astype(BF16).astype(F32), U32)
    hi = lax.bitcast_convert_type(v[:, c:].astype(BF16).astype(F32), U32)
    return (hi & jnp.uint32(0xFFFF0000)) | (lo >> 16)


def _unpack_rows(w):
    lo = lax.bitcast_convert_type(w << 16, F32)
    hi = lax.bitcast_convert_type(w & jnp.uint32(0xFFFF0000), F32)
    return jnp.concatenate([lo, hi], axis=1)


def _inproj_kernel(r_ref, g_ref, wa_ref, wb_ref, wf_ref, z_ref, f_ref, n_sc, *, na):
    j = pl.program_id(1)

    @pl.when(j == 0)
    def _():
        x = r_ref[...]
        ms = jnp.mean(x * x, axis=-1, keepdims=True)
        n = (x * lax.rsqrt(ms + RMS_EPS) * g_ref[...]).astype(BF16)
        n_sc[...] = n
        f_ref[...] = jnp.dot(n, wf_ref[...], preferred_element_type=F32)

    @pl.when(j < na)
    def _():
        z_ref[...] = jnp.dot(n_sc[...], wa_ref[...], preferred_element_type=F32).astype(z_ref.dtype)

    @pl.when(j >= na)
    def _():
        z_ref[...] = jnp.dot(n_sc[...], wb_ref[...], preferred_element_type=F32).astype(z_ref.dtype)


def _inproj(r, gain, w_a, w_b, wf):
    n_tok, d = r.shape
    tm, tn = INPROJ_TM, INPROJ_TN
    na, nb = w_a.shape[1] // tn, w_b.shape[1] // tn
    cols = (na + nb) * tn
    vmem = 2 * tm * d * 4 + 4 * d * tn * 2 + 2 * tm * tn * 2 + tm * d * 2 + 2 * (d * 2 + tm * 4) * LANES
    return pl.pallas_call(
        functools.partial(_inproj_kernel, na=na),
        grid=(n_tok // tm, na + nb),
        in_specs=[
            pl.BlockSpec((tm, d), lambda i, j: (i, 0)),
            pl.BlockSpec((1, d), lambda i, j: (0, 0)),
            pl.BlockSpec((d, tn), lambda i, j: (0, jnp.minimum(j, na - 1))),
            pl.BlockSpec((d, tn), lambda i, j: (0, jnp.maximum(j - na, 0))),
            pl.BlockSpec((d, LANES), lambda i, j: (0, 0)),
        ],
        out_specs=[
            pl.BlockSpec((tm, tn), lambda i, j: (i, j)),
            pl.BlockSpec((tm, LANES), lambda i, j: (i, 0)),
        ],
        out_shape=[
            jax.ShapeDtypeStruct((n_tok, cols), BF16),
            jax.ShapeDtypeStruct((n_tok, LANES), F32),
        ],
        scratch_shapes=[pltpu.VMEM((tm, d), BF16)],
        compiler_params=pltpu.CompilerParams(
            dimension_semantics=("parallel", "arbitrary"),
            vmem_limit_bytes=_vmem_limit(vmem + (8 << 20))),
        name="inproj",
    )(r, gain, w_a, w_b, wf)


def _cum_kernel(f_ref, b_ref, o_ref):
    ch = MXU_DIM
    lp = f_ref.shape[1]
    row = lax.broadcasted_iota(I32, (ch, ch), 0)
    col = lax.broadcasted_iota(I32, (ch, ch), 1)
    lower = (col <= row).astype(F32)
    carry = jnp.zeros((1, LANES), F32)
    for c in range(lp // ch):
        x = f_ref[0, c * ch:(c + 1) * ch, :] + b_ref[...]
        log_f = jnp.minimum(x, 0.0) - jnp.log1p(jnp.exp(-jnp.abs(x)))
        loc = jnp.dot(lower, log_f, preferred_element_type=F32,
                      precision=lax.Precision.HIGHEST) + carry
        o_ref[0, c * ch:(c + 1) * ch, :] = loc
        carry = loc[ch - 1:ch, :]


def _forget_cumsum(f3, b_forget):
    batch, lp, _ = f3.shape
    return pl.pallas_call(
        _cum_kernel,
        grid=(batch,),
        in_specs=[
            pl.BlockSpec((1, lp, LANES), lambda b: (b, 0, 0)),
            pl.BlockSpec((1, LANES), lambda b: (0, 0)),
        ],
        out_specs=pl.BlockSpec((1, lp, LANES), lambda b: (b, 0, 0)),
        out_shape=jax.ShapeDtypeStruct((batch, lp, LANES), F32),
        compiler_params=pltpu.CompilerParams(dimension_semantics=("parallel",)),
        name="forget_cumsum",
    )(f3, b_forget)


def _attn_kernel(q_ref, k_ref, v_ref, c_ref, o_ref, kaug_sc, vt_sc, qaug_sc, s_a, s_b,
                 m_sc, l_sc, acc_sc):
    tq = q_ref.shape[1]
    tk = tq
    lp = k_ref.shape[1]
    hg = pl.program_id(1)
    qi = pl.program_id(2)
    lane = lax.broadcasted_iota(I32, (1, LANES), 1)
    nt_dims = (((1,), (1,)), ((), ()))

    def own_lanes(h):
        return lane < HEAD_DIM if h % HEADS_PER_BLOCK == 0 else lane >= HEAD_DIM

    def bias_lane(h):
        return HEAD_DIM if h % HEADS_PER_BLOCK == 0 else 0

    def block_lanes(h):
        blk = h // HEADS_PER_BLOCK
        return slice(blk * LANES, (blk + 1) * LANES)

    @pl.when(qi == 0)
    def _():
        for c in range(lp // tk):
            rows = slice(c * tk, (c + 1) * tk)
            vt_sc[:, rows] = v_ref[0, rows, :].astype(F32).T.astype(BF16)
            cum = c_ref[0, rows, :]
            key_pos = c * tk + lax.broadcasted_iota(I32, (tk, 1), 0)
            for h in range(ATT_HEADS):
                head = hg * ATT_HEADS + h
                col = jnp.sum(jnp.where(lane == head, cum, 0.0), axis=-1, keepdims=True) * LOG2_E
                hi = col.astype(BF16).astype(F32)
                rem = col - hi
                mid = rem.astype(BF16).astype(F32)
                low = rem - mid
                hi = jnp.where(key_pos < META0, -MASK_VALUE, hi)
                a = bias_lane(h)
                bias = jnp.where(lane == a, hi, jnp.where(lane == a + 1, mid,
                                                          jnp.where(lane == a + 2, low, 0.0)))
                kaug_sc[h, rows, :] = jnp.where(own_lanes(h), k_ref[0, rows, block_lanes(h)],
                                                bias.astype(BF16))

    for h in range(ATT_HEADS):
        a = bias_lane(h)
        minus_one = jnp.where((lane >= a) & (lane < a + 3), -1.0, 0.0)
        scaled = q_ref[0, :, block_lanes(h)].astype(F32) * (HEAD_DIM ** -0.5 * LOG2_E)
        qaug_sc[h] = jnp.where(own_lanes(h), scaled, minus_one).astype(BF16)
        m_sc[h] = jnp.full((1, tq), MASK_VALUE, F32)
        l_sc[h] = jnp.zeros((1, tq), F32)
        acc_sc[h] = jnp.zeros((HEAD_DIM, tq), F32)

    def scores_into(dst, j):
        s0 = pl.multiple_of(j * tk, tk)
        for h in range(ATT_HEADS):
            dst[h] = lax.dot_general(kaug_sc[h, pl.ds(s0, tk), :], qaug_sc[h], nt_dims,
                                     preferred_element_type=F32)

    def softmax_pv(src, j, diagonal):
        s0 = pl.multiple_of(j * tk, tk)
        for h in range(ATT_HEADS):
            st = src[h]
            if diagonal:
                key = lax.broadcasted_iota(I32, (tk, tq), 0)
                qry = lax.broadcasted_iota(I32, (tk, tq), 1)
                st = jnp.where(key <= qry, st, MASK_VALUE)
            m = m_sc[h]
            m_new = jnp.maximum(m, jnp.max(st, axis=0, keepdims=True))
            alpha = jnp.exp2(m - m_new)
            p = jnp.exp2(st - m_new)
            l_sc[h] = alpha * l_sc[h] + jnp.sum(p, axis=0, keepdims=True)
            pv = jnp.dot(vt_sc[h * HEAD_DIM:(h + 1) * HEAD_DIM, pl.ds(s0, tk)], p.astype(BF16),
                         preferred_element_type=F32)
            acc_sc[h] = alpha * acc_sc[h] + pv
            m_sc[h] = m_new

    scores_into(s_a, 0)

    def pair(i, _):
        j = 2 * i
        scores_into(s_b, j + 1)
        softmax_pv(s_a, j, False)
        scores_into(s_a, j + 2)
        softmax_pv(s_b, j + 1, False)
        return 0

    lax.fori_loop(0, qi // 2, pair, 0)

    @pl.when(qi % 2 == 0)
    def _():
        softmax_pv(s_a, qi, True)

    @pl.when(qi % 2 == 1)
    def _():
        scores_into(s_b, qi)
        softmax_pv(s_a, qi - 1, False)
        softmax_pv(s_b, qi, True)

    att_t = jnp.concatenate([acc_sc[h] / l_sc[h] for h in range(ATT_HEADS)], axis=0)
    o_ref[0] = att_t.T.astype(o_ref.dtype)


def _attention(z3, cum, col_q, col_k, col_v):
    batch, lp, _ = z3.shape
    att_w = col_k - col_q
    tq = SEQ_TILE
    qb, kb, vb = col_q // ATT_BLOCK, col_k // ATT_BLOCK, col_v // ATT_BLOCK
    return pl.pallas_call(
        _attn_kernel,
        grid=(batch, att_w // ATT_BLOCK, lp // tq),
        in_specs=[
            pl.BlockSpec((1, tq, ATT_BLOCK), lambda b, h, i: (b, i, qb + h)),
            pl.BlockSpec((1, lp, ATT_BLOCK), lambda b, h, i: (b, 0, kb + h)),
            pl.BlockSpec((1, lp, ATT_BLOCK), lambda b, h, i: (b, 0, vb + h)),
            pl.BlockSpec((1, lp, LANES), lambda b, h, i: (b, 0, 0)),
        ],
        out_specs=pl.BlockSpec((1, tq, ATT_BLOCK), lambda b, h, i: (b, i, h)),
        out_shape=jax.ShapeDtypeStruct((batch, lp, att_w), BF16),
        scratch_shapes=[
            pltpu.VMEM((ATT_HEADS, lp, LANES), BF16), pltpu.VMEM((ATT_BLOCK, lp), BF16),
            pltpu.VMEM((ATT_HEADS, tq, LANES), BF16),
            pltpu.VMEM((ATT_HEADS, tq, tq), F32), pltpu.VMEM((ATT_HEADS, tq, tq), F32),
            pltpu.VMEM((ATT_HEADS, 1, tq), F32), pltpu.VMEM((ATT_HEADS, 1, tq), F32),
            pltpu.VMEM((ATT_HEADS, HEAD_DIM, tq), F32),
        ],
        compiler_params=pltpu.CompilerParams(
            dimension_semantics=("parallel", "parallel", "arbitrary")),
        name="fox_attention",
    )(z3, z3, z3, cum)


def _conv_kernel(a_ref, b_ref, w_ref, cb_ref, g_ref, lb_ref, o_ref, zbuf, ybuf, zshift):
    tt = a_ref.shape[1]
    ch = a_ref.shape[2]
    t = pl.program_id(1)

    @pl.when(t == 0)
    def _():
        zbuf[0:CONV_HALO, :] = jnp.zeros((CONV_HALO, ch), F32)

    z = a_ref[0].astype(F32) * _sigmoid(b_ref[0].astype(F32))
    rows = t * tt + lax.broadcasted_iota(I32, (tt, 1), 0)
    zbuf[CONV_HALO:CONV_HALO + tt, :] = jnp.where(rows >= META0, z, 0.0)

    first = CONV_HALO - (CONV_K - 1)

    def chan_block(cb, _):
        c0 = pl.multiple_of(cb * LANES, LANES)
        for res in range(SUBLANES):
            span = tt + ((CONV_K - 1 - res) // SUBLANES) * SUBLANES
            zshift[res, 0:span, :] = zbuf[first + res:first + res + span, pl.ds(c0, LANES)]
        acc = jnp.zeros((tt, LANES), F32) + cb_ref[:, pl.ds(c0, LANES)]
        for k in range(CONV_K):
            base = (k // SUBLANES) * SUBLANES
            acc = acc + w_ref[k:k + 1, pl.ds(c0, LANES)] * zshift[k % SUBLANES, base:base + tt, :]
        ybuf[:, pl.ds(c0, LANES)] = acc
        return 0

    lax.fori_loop(0, ch // LANES, chan_block, 0)

    y = ybuf[...]
    mu = jnp.mean(y, axis=-1, keepdims=True)
    yc = y - mu
    var = jnp.mean(yc * yc, axis=-1, keepdims=True)
    zn = yc * lax.rsqrt(var + LN_EPS) * g_ref[...] + lb_ref[...]
    o_ref[0] = (zn * _sigmoid(zn)).astype(o_ref.dtype)
    zbuf[0:CONV_HALO, :] = zbuf[tt:tt + CONV_HALO, :]


def _conv_branch(z3, col_a, col_b, conv_w, conv_b, ln_g, ln_b):
    batch, lp, _ = z3.shape
    ch = conv_b.shape[1]
    tt = SEQ_TILE
    ab, bb = col_a // ch, col_b // ch
    return pl.pallas_call(
        _conv_kernel,
        grid=(batch, lp // tt),
        in_specs=[
            pl.BlockSpec((1, tt, ch), lambda b, t: (b, t, ab)),
            pl.BlockSpec((1, tt, ch), lambda b, t: (b, t, bb)),
            pl.BlockSpec(conv_w.shape, lambda b, t: (0, 0)),
            pl.BlockSpec((1, ch), lambda b, t: (0, 0)),
            pl.BlockSpec((1, ch), lambda b, t: (0, 0)),
            pl.BlockSpec((1, ch), lambda b, t: (0, 0)),
        ],
        out_specs=pl.BlockSpec((1, tt, ch), lambda b, t: (b, t, 0)),
        out_shape=jax.ShapeDtypeStruct((batch, lp, ch), BF16),
        scratch_shapes=[
            pltpu.VMEM((tt + CONV_HALO, ch), F32), pltpu.VMEM((tt, ch), F32),
            pltpu.VMEM((SUBLANES, tt + ((CONV_K - 1) // SUBLANES) * SUBLANES, LANES), F32),
        ],
        compiler_params=pltpu.CompilerParams(dimension_semantics=("parallel", "arbitrary")),
        name="conv_branch",
    )(z3, z3, conv_w, conv_b, ln_g, ln_b)


def _merge_route_kernel(att_ref, cv_ref, ga0_ref, ga1_ref, gc0_ref, gc1_ref, r_ref, wao_ref, wco_ref,
                        wo_ref, g_ref, wr_ref, br_ref, ro_ref, n_ref, route_ref, cnt_ref, cnt_sc,
                        *, tiles_per_seq):
    i = pl.program_id(0)
    tm = r_ref.shape[0]

    @pl.when(i == 0)
    def _():
        cnt_sc[...] = jnp.zeros_like(cnt_sc)

    br_att = jnp.dot(att_ref[...], wao_ref[...], preferred_element_type=F32)
    br_conv = jnp.dot(cv_ref[...], wco_ref[...], preferred_element_type=F32)
    g_att = jnp.concatenate([ga0_ref[...], ga1_ref[...]], axis=1).astype(F32)
    g_conv = jnp.concatenate([gc0_ref[...], gc1_ref[...]], axis=1).astype(F32)
    merged = _sigmoid(g_att) * br_att + _sigmoid(g_conv) * br_conv
    r_new = r_ref[...] + jnp.dot(merged.astype(BF16), wo_ref[...], preferred_element_type=F32)
    ro_ref[...] = r_new

    ms = jnp.mean(r_new * r_new, axis=-1, keepdims=True)
    n = r_new * lax.rsqrt(ms + RMS_EPS) * g_ref[...]
    n_ref[...] = _pack_rows(n)

    logits = jnp.dot(n.astype(BF16), wr_ref[...], preferred_element_type=F32) + br_ref[...]
    lane = lax.broadcasted_iota(I32, logits.shape, 1)
    big = jnp.int32(4 * LANES)
    first_e = ROUTE_FIRST_EXPERT_LANE

    gl = jnp.where(lane < N_GROUPS, logits, -jnp.inf)
    gmax = jnp.max(gl, axis=-1, keepdims=True)
    gsum = jnp.sum(jnp.exp(gl - gmax), axis=-1, keepdims=True)
    g_w = 1.0 / gsum
    g_idx = jnp.min(jnp.where(gl == gmax, lane, big), axis=-1, keepdims=True)

    in_group = (lane >= first_e + g_idx * EXPERTS_PER_GROUP) & \
               (lane < first_e + (g_idx + 1) * EXPERTS_PER_GROUP)
    el = jnp.where(in_group, logits, -jnp.inf)
    v0 = jnp.max(el, axis=-1, keepdims=True)
    i0 = jnp.min(jnp.where(el == v0, lane, big), axis=-1, keepdims=True)
    el = jnp.where(lane == i0, -jnp.inf, el)
    v1 = jnp.max(el, axis=-1, keepdims=True)
    i1 = jnp.min(jnp.where(el == v1, lane, big), axis=-1, keepdims=True)
    e1 = jnp.exp(v1 - v0)
    w0 = g_w / (1.0 + e1)
    w1 = g_w * e1 / (1.0 + e1)

    pos = (i % tiles_per_seq) * tm + lax.broadcasted_iota(I32, (tm, 1), 0)
    valid = pos >= META0
    onehot = jnp.where(((lane == i0) | (lane == i1)) & valid, 1.0, 0.0)
    rr = lax.broadcasted_iota(I32, (tm, tm), 0)
    cc = lax.broadcasted_iota(I32, (tm, tm), 1)
    lower = jnp.where(cc < rr, 1.0, 0.0).astype(BF16)
    rank = jnp.dot(lower, onehot.astype(BF16), preferred_element_type=F32) + cnt_sc[0:1, :]
    rank0 = jnp.sum(jnp.where(lane == i0, rank, 0.0), axis=-1, keepdims=True)
    rank1 = jnp.sum(jnp.where(lane == i1, rank, 0.0), axis=-1, keepdims=True)
    cnt_sc[0:1, :] = cnt_sc[0:1, :] + jnp.sum(onehot, axis=0, keepdims=True)
    cnt_ref[...] = jnp.broadcast_to(cnt_sc[0:1, :], cnt_ref.shape)

    w0 = jnp.where(valid, w0, 0.0)
    w1 = jnp.where(valid, w1, 0.0)
    out = jnp.where(lane == 0, (i0 - first_e).astype(F32), 0.0)
    out = jnp.where(lane == 1, (i1 - first_e).astype(F32), out)
    out = jnp.where(lane == 2, w0, out)
    out = jnp.where(lane == 3, w1, out)
    out = jnp.where(lane == 4, rank0, out)
    out = jnp.where(lane == 5, rank1, out)
    route_ref[...] = out


def _merge_route(att, cv, z, col_ga, col_gc, r, wao, wco, wo, gain, wr, br, tiles_per_seq):
    n_tok, d = r.shape
    aw = att.shape[1]
    cw = cv.shape[1]
    tm = SEQ_TILE
    half = d // 2
    gab, gcb = col_ga // half, col_gc // half
    const = lambda i: (0, 0)
    single = pl.Buffered(1)
    vmem = (2 * tm * (aw * 2 + cw * 2 + 2 * d * 2 + 3 * d * 4 + LANES * 4)
            + (aw * d + cw * d + d * d + d * LANES) * 2 + 8 * tm * d * 4)
    return pl.pallas_call(
        functools.partial(_merge_route_kernel, tiles_per_seq=tiles_per_seq),
        grid=(n_tok // tm,),
        in_specs=[
            pl.BlockSpec((tm, aw), lambda i: (i, 0)),
            pl.BlockSpec((tm, cw), lambda i: (i, 0)),
            pl.BlockSpec((tm, half), lambda i: (i, gab)),
            pl.BlockSpec((tm, half), lambda i: (i, gab + 1)),
            pl.BlockSpec((tm, half), lambda i: (i, gcb)),
            pl.BlockSpec((tm, half), lambda i: (i, gcb + 1)),
            pl.BlockSpec((tm, d), lambda i: (i, 0)),
            pl.BlockSpec(wao.shape, const, pipeline_mode=single),
            pl.BlockSpec(wco.shape, const, pipeline_mode=single),
            pl.BlockSpec(wo.shape, const, pipeline_mode=single),
            pl.BlockSpec((1, d), const),
            pl.BlockSpec(wr.shape, const, pipeline_mode=single),
            pl.BlockSpec((1, LANES), const),
        ],
        out_specs=[
            pl.BlockSpec((tm, d), lambda i: (i, 0)),
            pl.BlockSpec((tm, d // 2), lambda i: (i, 0)),
            pl.BlockSpec((tm, LANES), lambda i: (i, 0)),
            pl.BlockSpec((8, LANES), const),
        ],
        out_shape=[
            jax.ShapeDtypeStruct((n_tok, d), F32),
            jax.ShapeDtypeStruct((n_tok, d // 2), U32),
            jax.ShapeDtypeStruct((n_tok, LANES), F32),
            jax.ShapeDtypeStruct((8, LANES), F32),
        ],
        scratch_shapes=[pltpu.VMEM((8, LANES), F32)],
        compiler_params=pltpu.CompilerParams(
            dimension_semantics=("arbitrary",), vmem_limit_bytes=_vmem_limit(vmem)),
        name="merge_route",
    )(att, cv, z, z, z, z, r, wao, wco, wo, gain, wr, br)


def _dispatch_kernel(dest_ref, zflag_ref, n_ref, xs_ref, zero_sc, stage, sem, zsem, *, tiles_per_seq):
    i = pl.program_id(0)
    tm = zero_sc.shape[0]
    n_tiles = xs_ref.shape[0] // tm

    def zero_copy(t):
        return pltpu.make_async_copy(zero_sc, xs_ref.at[pl.ds(pl.multiple_of(t * tm, tm), tm)], zsem)

    @pl.when(i == 0)
    def _():
        zero_sc[...] = jnp.zeros_like(zero_sc)

        def issue_zero(t, _):
            @pl.when(zflag_ref[t] != 0)
            def _():
                zero_copy(t).start()
            return 0

        def drain_zero(t, _):
            @pl.when(zflag_ref[t] != 0)
            def _():
                zero_copy(t).wait()
            return 0

        lax.fori_loop(0, n_tiles, issue_zero, 0)
        lax.fori_loop(0, n_tiles, drain_zero, 0)

    slot = i % 2
    last = pl.num_programs(0) - 1

    def first_row_group(step):
        return jnp.where(step % tiles_per_seq == 0, META0, 0) // SUBLANES

    def row_copy(step, buf, g, u, k):
        d = dest_ref[(step * tm + g * SUBLANES + u) * 2 + k]
        return pltpu.make_async_copy(stage.at[buf, g, pl.ds(u, 1)], xs_ref.at[pl.ds(d, 1)], sem.at[buf])

    def drain(step, buf):
        one_row = pltpu.make_async_copy(stage.at[buf, 0, pl.ds(0, 1)], xs_ref.at[pl.ds(0, 1)], sem.at[buf])

        def body(g, _):
            for _u in range(2 * SUBLANES):
                one_row.wait()
            return 0
        lax.fori_loop(first_row_group(step), tm // SUBLANES, body, 0)

    @pl.when(i >= 2)
    def _():
        drain(i - 2, slot)

    stage[slot] = n_ref[...]

    def issue(g, _):
        for u in range(SUBLANES):
            row_copy(i, slot, g, u, 0).start()
            row_copy(i, slot, g, u, 1).start()
        return 0

    lax.fori_loop(first_row_group(i), tm // SUBLANES, issue, 0)

    @pl.when(i == last)
    def _():
        @pl.when(i >= 1)
        def _():
            drain(i - 1, 1 - slot)
        drain(i, slot)


def _dispatch(dest, zflag, n, n_rows, tiles_per_seq):
    n_tok, w = n.shape
    tm = MOE_TM
    groups = tm // SUBLANES
    return pl.pallas_call(
        functools.partial(_dispatch_kernel, tiles_per_seq=tiles_per_seq),
        grid_spec=pltpu.PrefetchScalarGridSpec(
            num_scalar_prefetch=2,
            grid=(n_tok // tm,),
            in_specs=[pl.BlockSpec((groups, SUBLANES, w), lambda i, *_: (i, 0, 0))],
            out_specs=pl.BlockSpec(memory_space=pl.ANY),
            scratch_shapes=[pltpu.VMEM((tm, w), n.dtype), pltpu.VMEM((2, groups, SUBLANES, w), n.dtype),
                            pltpu.SemaphoreType.DMA((2,)), pltpu.SemaphoreType.DMA],
        ),
        out_shape=jax.ShapeDtypeStruct((n_rows, w), n.dtype),
        compiler_params=pltpu.CompilerParams(dimension_semantics=("arbitrary",),
                                             has_side_effects=True),
        name="moe_dispatch",
    )(dest, zflag, n.reshape(n_tok // SUBLANES, SUBLANES, w))


def _expert_kernel(te_ref, first_ref, nxt_ref, na_ref, x_ref, wgu_hbm, wdn_hbm, y_ref,
                   gu_stage, dn_stage, gu_bf, dn_bf, sem, *, layer):
    i = pl.program_id(0)
    active = i < na_ref[0]

    def weight_copies(e):
        return (pltpu.make_async_copy(wgu_hbm.at[layer, e], gu_stage, sem.at[0]),
                pltpu.make_async_copy(wdn_hbm.at[layer, e], dn_stage, sem.at[1]))

    @pl.when(i == 0)
    def _():
        for cp in weight_copies(te_ref[0]):
            cp.start()

    @pl.when(active & (first_ref[i] != 0))
    def _():
        for cp in weight_copies(te_ref[i]):
            cp.wait()
        gu_bf[...] = gu_stage[...].astype(BF16)
        dn_bf[...] = dn_stage[...].astype(BF16)

        @pl.when(nxt_ref[i] >= 0)
        def _():
            for cp in weight_copies(nxt_ref[i]):
                cp.start()

    @pl.when(active)
    def _():
        de = dn_bf.shape[0]
        x = _unpack_rows(x_ref[...]).astype(BF16)
        h = jnp.dot(x, gu_bf[...], preferred_element_type=F32)
        a = h[:, :de]
        b = h[:, de:]
        act = a * _sigmoid(a) * b
        y_ref[...] = _pack_rows(jnp.dot(act.astype(BF16), dn_bf[...], preferred_element_type=F32))

    @pl.when(jnp.logical_not(active))
    def _():
        y_ref[...] = jnp.zeros_like(y_ref)


def _experts(tile_e, first, nxt, n_act, xs, w_gate_up, w_down, layer):
    n_rows, w = xs.shape
    d = w_gate_up.shape[2]
    assert d == 2 * w
    tm = MOE_TM
    de2 = w_gate_up.shape[3]
    de = w_down.shape[2]
    row_map = lambda i, te, fi, nx, na: (jnp.minimum(i, na[0] - 1), 0)
    vmem = (d * de2 + de * d) * (4 + 2) + 6 * tm * d * 4 + 6 * tm * de2 * 4
    return pl.pallas_call(
        functools.partial(_expert_kernel, layer=layer),
        grid_spec=pltpu.PrefetchScalarGridSpec(
            num_scalar_prefetch=4,
            grid=(n_rows // tm,),
            in_specs=[
                pl.BlockSpec((tm, w), row_map),
                pl.BlockSpec(memory_space=pl.ANY),
                pl.BlockSpec(memory_space=pl.ANY),
            ],
            out_specs=pl.BlockSpec((tm, w), lambda i, *_: (i, 0)),
            scratch_shapes=[
                pltpu.VMEM((d, de2), F32), pltpu.VMEM((de, d), F32),
                pltpu.VMEM((d, de2), BF16), pltpu.VMEM((de, d), BF16),
                pltpu.SemaphoreType.DMA((2,)),
            ],
        ),
        out_shape=jax.ShapeDtypeStruct((n_rows, w), xs.dtype),
        compiler_params=pltpu.CompilerParams(
            dimension_semantics=("arbitrary",), vmem_limit_bytes=_vmem_limit(vmem)),
        name="moe_experts",
    )(tile_e, first, nxt, n_act, xs, w_gate_up, w_down)


def _combine_kernel(dest_ref, r_ref, route_ref, y_ref, *rest, tile_of, final):
    if final:
        g_ref, o_ref, ybuf, sem = rest
    else:
        o_ref, ybuf, sem = rest
    tm = r_ref.shape[0]
    i = pl.program_id(0)
    slot = i % 2

    def issue(step, buf):
        tile = tile_of(step)

        def body(g, _):
            for u in range(SUBLANES):
                for k in range(2):
                    d = dest_ref[(tile * tm + g * SUBLANES + u) * 2 + k]
                    pltpu.make_async_copy(y_ref.at[pl.ds(d, 1)], ybuf.at[buf, k, g, pl.ds(u, 1)],
                                          sem.at[buf]).start()
            return 0
        lax.fori_loop(0, tm // SUBLANES, body, 0)

    @pl.when(i == 0)
    def _():
        issue(i, slot)

    @pl.when(i + 1 < pl.num_programs(0))
    def _():
        issue(i + 1, 1 - slot)

    one_row = pltpu.make_async_copy(y_ref.at[pl.ds(0, 1)], ybuf.at[slot, 0, 0, pl.ds(0, 1)], sem.at[slot])

    def drain(g, _):
        for _u in range(2 * SUBLANES):
            one_row.wait()
        return 0
    lax.fori_loop(0, tm // SUBLANES, drain, 0)

    route = route_ref[...]
    w = ybuf.shape[-1]
    out = (r_ref[...] + route[:, 2:3] * _unpack_rows(ybuf[slot, 0].reshape(tm, w))
           + route[:, 3:4] * _unpack_rows(ybuf[slot, 1].reshape(tm, w)))
    if final:
        ms = jnp.mean(out * out, axis=-1, keepdims=True)
        out = out * lax.rsqrt(ms + RMS_EPS) * g_ref[...]
    o_ref[...] = out


def _combine(dest, r, route, y, tiles_per_seq, final_gain=None):
    n_tok, d = r.shape
    tm = MOE_TM
    final = final_gain is not None
    if final:
        real_tiles = tiles_per_seq - P0 // tm
        tile_of = lambda i: (i // real_tiles) * tiles_per_seq + P0 // tm + i % real_tiles
        n_out_tiles = (n_tok // tm // tiles_per_seq) * real_tiles
    else:
        tile_of = lambda i: i
        n_out_tiles = n_tok // tm
    in_specs = [
        pl.BlockSpec((tm, d), lambda i, *_: (tile_of(i), 0)),
        pl.BlockSpec((tm, LANES), lambda i, *_: (tile_of(i), 0)),
        pl.BlockSpec(memory_space=pl.ANY),
    ]
    args = [dest, r, route, y]
    if final:
        in_specs.append(pl.BlockSpec((1, d), lambda i, *_: (0, 0)))
        args.append(final_gain)
    return pl.pallas_call(
        functools.partial(_combine_kernel, tile_of=tile_of, final=final),
        grid_spec=pltpu.PrefetchScalarGridSpec(
            num_scalar_prefetch=1,
            grid=(n_out_tiles,),
            in_specs=in_specs,
            out_specs=pl.BlockSpec((tm, d), lambda i, *_: (i, 0)),
            scratch_shapes=[pltpu.VMEM((2, 2, tm // SUBLANES, SUBLANES, d // 2), y.dtype),
                            pltpu.SemaphoreType.DMA((2,))],
        ),
        out_shape=jax.ShapeDtypeStruct((n_out_tiles * tm, d), F32),
        compiler_params=pltpu.CompilerParams(
            dimension_semantics=("arbitrary",),
            vmem_limit_bytes=_vmem_limit(8 * tm * d * 4 + 4 * tm * d * 4 + (8 << 20))),
        name="moe_combine_final" if final else "moe_combine",
    )(*args)


def _routing_tables(route, counts, n_tiles_max, valid_tok):
    tm = MOE_TM
    cnt = counts[0, ROUTE_FIRST_EXPERT_LANE:ROUTE_FIRST_EXPERT_LANE + N_EXPERTS].astype(I32)
    ntile = (cnt + tm - 1) // tm
    tile_end = jnp.cumsum(ntile)
    tile_start = tile_end - ntile
    n_act = tile_end[-1:]
    e_ids = route[:, 0:2].astype(I32)
    rank = route[:, 4:6].astype(I32)
    dest = jnp.where(valid_tok[:, None], tile_start[e_ids] * tm + rank, 0).reshape(-1)
    all_tiles = jnp.arange(n_tiles_max, dtype=I32)
    clamped = jnp.minimum(all_tiles, n_act[0] - 1)
    tile_e = jnp.sum((clamped[:, None] >= tile_end[None, :]).astype(I32), axis=1)
    tile_e = jnp.minimum(tile_e, N_EXPERTS - 1)
    first = jnp.concatenate([jnp.ones((1,), I32), (tile_e[1:] != tile_e[:-1]).astype(I32)])
    end_of_mine = tile_end[tile_e]
    nxt = jnp.where(end_of_mine < n_act[0], tile_e[jnp.minimum(end_of_mine, n_tiles_max - 1)], -1)
    is_expert_tail = jnp.any((all_tiles[:, None] == tile_end[None, :] - 1) & (ntile[None, :] > 0), axis=1)
    zflag = (is_expert_tail | (all_tiles >= n_act[0])).astype(I32)
    return dest, tile_e, first, nxt.astype(I32), n_act.astype(I32), zflag


def kernel(x, meta, norm_mix, w_in, b_forget, w_attn_out, conv_w, conv_b, conv_ln_g, conv_ln_b,
           w_conv_out, w_out, norm_ffn, w_router_group, b_router_group, w_router_expert,
           b_router_expert, w_gate_up, w_down, norm_final):
    batch, seq, d = x.shape
    depth = w_in.shape[0]
    heads = b_forget.shape[1]
    att_w = heads * HEAD_DIM
    conv_c = conv_b.shape[1]
    lp = P0 + seq
    tiles_per_seq = lp // SEQ_TILE
    n_tok = batch * lp
    assert seq % SEQ_TILE == 0 and n_tok % INPROJ_TM == 0
    assert (3 * att_w) % INPROJ_TN == 0 and (2 * conv_c + 2 * d) % INPROJ_TN == 0 and d % 2 == 0
    assert att_w % LANES == 0 and conv_c % LANES == 0 and SEQ_TILE == MOE_TM

    front = jnp.concatenate([jnp.zeros((META0, d), x.dtype), meta.astype(x.dtype)], axis=0)
    r = jnp.concatenate([jnp.broadcast_to(front[None], (batch, P0, d)), x], axis=1).reshape(n_tok, d)

    c_f = 3 * att_w
    c_u = c_f + heads
    m_q, m_k, m_v = 0, att_w, 2 * att_w
    m_ua = 3 * att_w
    m_ub = m_ua + conv_c
    m_ga = m_ub + conv_c
    m_gc = m_ga + d

    valid_tok = jnp.tile(jnp.arange(lp) >= META0, batch)
    n_pairs = 2 * batch * (lp - META0)
    n_tiles_max = n_pairs // MOE_TM + N_EXPERTS
    n_rows = n_tiles_max * MOE_TM

    out = None
    for l in range(depth):
        wl = w_in[l]
        wf = jnp.pad(wl[:, c_f:c_u], ((0, 0), (0, LANES - heads))).astype(BF16)
        z, f = _inproj(r, norm_mix[l][None], wl[:, :c_f].astype(BF16), wl[:, c_u:].astype(BF16), wf)
        cum = _forget_cumsum(f.reshape(batch, lp, LANES),
                             jnp.pad(b_forget[l], (0, LANES - heads))[None])
        z3 = z.reshape(batch, lp, -1)
        att = _attention(z3, cum, m_q, m_k, m_v).reshape(n_tok, att_w)
        conv_w_pad = jnp.concatenate([conv_w[l], jnp.zeros((1, conv_c), F32)], axis=0)
        cv = _conv_branch(z3, m_ua, m_ub, conv_w_pad, conv_b[l][None], conv_ln_g[l][None],
                          conv_ln_b[l][None]).reshape(n_tok, conv_c)

        wr = jnp.zeros((d, LANES), F32)
        wr = wr.at[:, :N_GROUPS].set(w_router_group[l])
        wr = wr.at[:, N_GROUPS:N_GROUPS + N_EXPERTS].set(w_router_expert[l]).astype(BF16)
        br = jnp.zeros((1, LANES), F32)
        br = br.at[0, :N_GROUPS].set(b_router_group[l])
        br = br.at[0, N_GROUPS:N_GROUPS + N_EXPERTS].set(b_router_expert[l])
        r, n2, route, counts = _merge_route(
            att, cv, z, m_ga, m_gc, r, w_attn_out[l].astype(BF16), w_conv_out[l].astype(BF16),
            w_out[l].astype(BF16), norm_ffn[l][None], wr, br, tiles_per_seq)

        dest, tile_e, first, nxt, n_act, zflag = _routing_tables(route, counts, n_tiles_max, valid_tok)
        xs = _dispatch(dest, zflag, n2, n_rows, tiles_per_seq)
        y = _experts(tile_e, first, nxt, n_act, xs, w_gate_up, w_down, l)
        if l + 1 < depth:
            r = _combine(dest, r, route, y, tiles_per_seq)
        else:
            out = _combine(dest, r, route, y, tiles_per_seq, final_gain=norm_final[None])
    return out.reshape(batch, seq, d)
```

```python
import functools

import jax
import jax.numpy as jnp
from jax import lax
from jax.experimental import pallas as pl
from jax.experimental.pallas import tpu as pltpu

F32 = jnp.float32
BF16 = jnp.bfloat16
I32 = jnp.int32
U32 = jnp.uint32

LANES = 128
SUBLANES = 8
MXU_DIM = 256
VMEM_BYTES_V7X = 64 * 1024 * 1024

N_META = 16
HEAD_DIM = 64
HEADS_PER_BLOCK = LANES // HEAD_DIM
ATT_HEADS = 8
ATT_BLOCK = ATT_HEADS * HEAD_DIM
CONV_K = 31
N_GROUPS = 4
EXPERTS_PER_GROUP = 8
N_EXPERTS = N_GROUPS * EXPERTS_PER_GROUP
LOG2_E = 1.4426950408889634
RMS_EPS = 1e-6
LN_EPS = 1e-5

SEQ_TILE = 256
P0 = SEQ_TILE
META0 = P0 - N_META
INPROJ_TM = 1024
INPROJ_TN = 1024
MOE_TM = 256
CONV_HALO = 32
DMA_UNROLL = 8
ROUTE_FIRST_EXPERT_LANE = N_GROUPS
MASK_VALUE = -1e30


def _vmem_limit(nbytes):
    return int(min(max(nbytes, 16 * 1024 * 1024), VMEM_BYTES_V7X - 6 * 1024 * 1024))


def _sigmoid(x):
    return 1.0 / (1.0 + jnp.exp(-x))


def _pack_rows(v):
    c = v.shape[1] // 2
    lo = lax.bitcast_convert_type(v[:, :c].astype(BF16).astype(F32), U32)
    hi = lax.bitcast_convert_type(v[:, c:].astype(BF16).astype(F32), U32)
    return (hi & jnp.uint32(0xFFFF0000)) | (lo >> 16)


def _unpack_rows(w):
    lo = lax.bitcast_convert_type(w << 16, F32)
    hi = lax.bitcast_convert_type(w & jnp.uint32(0xFFFF0000), F32)
    return jnp.concatenate([lo, hi], axis=1)


def _inproj_kernel(r_ref, g_ref, wa_ref, wb_ref, wf_ref, z_ref, f_ref, n_sc, *, na):
    j = pl.program_id(1)

    @pl.when(j == 0)
    def _():
        x = r_ref[...]
        ms = jnp.mean(x * x, axis=-1, keepdims=True)
        n = (x * lax.rsqrt(ms + RMS_EPS) * g_ref[...]).astype(BF16)
        n_sc[...] = n
        f_ref[...] = jnp.dot(n, wf_ref[...], preferred_element_type=F32)

    @pl.when(j < na)
    def _():
        z_ref[...] = jnp.dot(n_sc[...], wa_ref[...], preferred_element_type=F32).astype(z_ref.dtype)

    @pl.when(j >= na)
    def _():
        z_ref[...] = jnp.dot(n_sc[...], wb_ref[...], preferred_element_type=F32).astype(z_ref.dtype)


def _inproj(r, gain, w_a, w_b, wf):
    n_tok, d = r.shape
    tm, tn = INPROJ_TM, INPROJ_TN
    na, nb = w_a.shape[1] // tn, w_b.shape[1] // tn
    cols = (na + nb) * tn
    vmem = 2 * tm * d * 4 + 4 * d * tn * 2 + 2 * tm * tn * 2 + tm * d * 2 + 2 * (d * 2 + tm * 4) * LANES
    return pl.pallas_call(
        functools.partial(_inproj_kernel, na=na),
        grid=(n_tok // tm, na + nb),
        in_specs=[
            pl.BlockSpec((tm, d), lambda i, j: (i, 0)),
            pl.BlockSpec((1, d), lambda i, j: (0, 0)),
            pl.BlockSpec((d, tn), lambda i, j: (0, jnp.minimum(j, na - 1))),
            pl.BlockSpec((d, tn), lambda i, j: (0, jnp.maximum(j - na, 0))),
            pl.BlockSpec((d, LANES), lambda i, j: (0, 0)),
        ],
        out_specs=[
            pl.BlockSpec((tm, tn), lambda i, j: (i, j)),
            pl.BlockSpec((tm, LANES), lambda i, j: (i, 0)),
        ],
        out_shape=[
            jax.ShapeDtypeStruct((n_tok, cols), BF16),
            jax.ShapeDtypeStruct((n_tok, LANES), F32),
        ],
        scratch_shapes=[pltpu.VMEM((tm, d), BF16)],
        compiler_params=pltpu.CompilerParams(
            dimension_semantics=("parallel", "arbitrary"),
            vmem_limit_bytes=_vmem_limit(vmem + (8 << 20))),
        name="inproj",
    )(r, gain, w_a, w_b, wf)


def _cum_kernel(f_ref, b_ref, o_ref):
    ch = MXU_DIM
    lp = f_ref.shape[1]
    row = lax.broadcasted_iota(I32, (ch, ch), 0)
    col = lax.broadcasted_iota(I32, (ch, ch), 1)
    lower = (col <= row).astype(F32)
    carry = jnp.zeros((1, LANES), F32)
    for c in range(lp // ch):
        x = f_ref[0, c * ch:(c + 1) * ch, :] + b_ref[...]
        log_f = jnp.minimum(x, 0.0) - jnp.log1p(jnp.exp(-jnp.abs(x)))
        loc = jnp.dot(lower, log_f, preferred_element_type=F32,
                      precision=lax.Precision.HIGHEST) + carry
        o_ref[0, c * ch:(c + 1) * ch, :] = loc
        carry = loc[ch - 1:ch, :]


def _forget_cumsum(f3, b_forget):
    batch, lp, _ = f3.shape
    return pl.pallas_call(
        _cum_kernel,
        grid=(batch,),
        in_specs=[
            pl.BlockSpec((1, lp, LANES), lambda b: (b, 0, 0)),
            pl.BlockSpec((1, LANES), lambda b: (0, 0)),
        ],
        out_specs=pl.BlockSpec((1, lp, LANES), lambda b: (b, 0, 0)),
        out_shape=jax.ShapeDtypeStruct((batch, lp, LANES), F32),
        compiler_params=pltpu.CompilerParams(dimension_semantics=("parallel",)),
        name="forget_cumsum",
    )(f3, b_forget)


def _attn_kernel(q_ref, k_ref, v_ref, c_ref, o_ref, kaug_sc, vt_sc, qaug_sc, s_a, s_b,
                 m_sc, l_sc, acc_sc):
    tq = q_ref.shape[1]
    tk = tq
    lp = k_ref.shape[1]
    hg = pl.program_id(1)
    qi = pl.program_id(2)
    lane = lax.broadcasted_iota(I32, (1, LANES), 1)
    nt_dims = (((1,), (1,)), ((), ()))

    def own_lanes(h):
        return lane < HEAD_DIM if h % HEADS_PER_BLOCK == 0 else lane >= HEAD_DIM

    def bias_lane(h):
        return HEAD_DIM if h % HEADS_PER_BLOCK == 0 else 0

    def block_lanes(h):
        blk = h // HEADS_PER_BLOCK
        return slice(blk * LANES, (blk + 1) * LANES)

    @pl.when(qi == 0)
    def _():
        for c in range(lp // tk):
            rows = slice(c * tk, (c + 1) * tk)
            vt_sc[:, rows] = v_ref[0, rows, :].astype(F32).T.astype(BF16)
            cum = c_ref[0, rows, :]
            key_pos = c * tk + lax.broadcasted_iota(I32, (tk, 1), 0)
            for h in range(ATT_HEADS):
                head = hg * ATT_HEADS + h
                col = jnp.sum(jnp.where(lane == head, cum, 0.0), axis=-1, keepdims=True) * LOG2_E
                hi = col.astype(BF16).astype(F32)
                rem = col - hi
                mid = rem.astype(BF16).astype(F32)
                low = rem - mid
                hi = jnp.where(key_pos < META0, -MASK_VALUE, hi)
                a = bias_lane(h)
                bias = jnp.where(lane == a, hi, jnp.where(lane == a + 1, mid,
                                                          jnp.where(lane == a + 2, low, 0.0)))
                kaug_sc[h, rows, :] = jnp.where(own_lanes(h), k_ref[0, rows, block_lanes(h)],
                                                bias.astype(BF16))

    for h in range(ATT_HEADS):
        a = bias_lane(h)
        minus_one = jnp.where((lane >= a) & (lane < a + 3), -1.0, 0.0)
        scaled = q_ref[0, :, block_lanes(h)].astype(F32) * (HEAD_DIM ** -0.5 * LOG2_E)
        qaug_sc[h] = jnp.where(own_lanes(h), scaled, minus_one).astype(BF16)
        m_sc[h] = jnp.full((1, tq), MASK_VALUE, F32)
        l_sc[h] = jnp.zeros((1, tq), F32)
        acc_sc[h] = jnp.zeros((HEAD_DIM, tq), F32)

    def scores_into(dst, j):
        s0 = pl.multiple_of(j * tk, tk)
        for h in range(ATT_HEADS):
            dst[h] = lax.dot_general(kaug_sc[h, pl.ds(s0, tk), :], qaug_sc[h], nt_dims,
                                     preferred_element_type=F32)

    def softmax_pv(src, j, diagonal):
        s0 = pl.multiple_of(j * tk, tk)
        for h in range(ATT_HEADS):
            st = src[h]
            if diagonal:
                key = lax.broadcasted_iota(I32, (tk, tq), 0)
                qry = lax.broadcasted_iota(I32, (tk, tq), 1)
                st = jnp.where(key <= qry, st, MASK_VALUE)
            m = m_sc[h]
            m_new = jnp.maximum(m, jnp.max(st, axis=0, keepdims=True))
            alpha = jnp.exp2(m - m_new)
            p = jnp.exp2(st - m_new)
            l_sc[h] = alpha * l_sc[h] + jnp.sum(p, axis=0, keepdims=True)
            pv = jnp.dot(vt_sc[h * HEAD_DIM:(h + 1) * HEAD_DIM, pl.ds(s0, tk)], p.astype(BF16),
                         preferred_element_type=F32)
            acc_sc[h] = alpha * acc_sc[h] + pv
            m_sc[h] = m_new

    scores_into(s_a, 0)

    def pair(i, _):
        j = 2 * i
        scores_into(s_b, j + 1)
        softmax_pv(s_a, j, False)
        scores_into(s_a, j + 2)
        softmax_pv(s_b, j + 1, False)
        return 0

    lax.fori_loop(0, qi // 2, pair, 0)

    @pl.when(qi % 2 == 0)
    def _():
        softmax_pv(s_a, qi, True)

    @pl.when(qi % 2 == 1)
    def _():
        scores_into(s_b, qi)
        softmax_pv(s_a, qi - 1, False)
        softmax_pv(s_b, qi, True)

    att_t = jnp.concatenate([acc_sc[h] / l_sc[h] for h in range(ATT_HEADS)], axis=0)
    o_ref[0] = att_t.T.astype(o_ref.dtype)


def _attention(z3, cum, col_q, col_k, col_v):
    batch, lp, _ = z3.shape
    att_w = col_k - col_q
    tq = SEQ_TILE
    qb, kb, vb = col_q // ATT_BLOCK, col_k // ATT_BLOCK, col_v // ATT_BLOCK
    return pl.pallas_call(
        _attn_kernel,
        grid=(batch, att_w // ATT_BLOCK, lp // tq),
        in_specs=[
            pl.BlockSpec((1, tq, ATT_BLOCK), lambda b, h, i: (b, i, qb + h)),
            pl.BlockSpec((1, lp, ATT_BLOCK), lambda b, h, i: (b, 0, kb + h)),
            pl.BlockSpec((1, lp, ATT_BLOCK), lambda b, h, i: (b, 0, vb + h)),
            pl.BlockSpec((1, lp, LANES), lambda b, h, i: (b, 0, 0)),
        ],
        out_specs=pl.BlockSpec((1, tq, ATT_BLOCK), lambda b, h, i: (b, i, h)),
        out_shape=jax.ShapeDtypeStruct((batch, lp, att_w), BF16),
        scratch_shapes=[
            pltpu.VMEM((ATT_HEADS, lp, LANES), BF16), pltpu.VMEM((ATT_BLOCK, lp), BF16),
            pltpu.VMEM((ATT_HEADS, tq, LANES), BF16),
            pltpu.VMEM((ATT_HEADS, tq, tq), F32), pltpu.VMEM((ATT_HEADS, tq, tq), F32),
            pltpu.VMEM((ATT_HEADS, 1, tq), F32), pltpu.VMEM((ATT_HEADS, 1, tq), F32),
            pltpu.VMEM((ATT_HEADS, HEAD_DIM, tq), F32),
        ],
        compiler_params=pltpu.CompilerParams(
            dimension_semantics=("parallel", "parallel", "arbitrary")),
        name="fox_attention",
    )(z3, z3, z3, cum)


def _conv_kernel(a_ref, b_ref, w_ref, cb_ref, g_ref, lb_ref, o_ref, zbuf, ybuf, zshift):
    tt = a_ref.shape[1]
    ch = a_ref.shape[2]
    t = pl.program_id(1)

    @pl.when(t == 0)
    def _():
        zbuf[0:CONV_HALO, :] = jnp.zeros((CONV_HALO, ch), F32)

    z = a_ref[0].astype(F32) * _sigmoid(b_ref[0].astype(F32))
    rows = t * tt + lax.broadcasted_iota(I32, (tt, 1), 0)
    zbuf[CONV_HALO:CONV_HALO + tt, :] = jnp.where(rows >= META0, z, 0.0)

    first = CONV_HALO - (CONV_K - 1)

    def chan_block(cb, _):
        c0 = pl.multiple_of(cb * LANES, LANES)
        for res in range(SUBLANES):
            span = tt + ((CONV_K - 1 - res) // SUBLANES) * SUBLANES
            zshift[res, 0:span, :] = zbuf[first + res:first + res + span, pl.ds(c0, LANES)]
        acc = jnp.zeros((tt, LANES), F32) + cb_ref[:, pl.ds(c0, LANES)]
        for k in range(CONV_K):
            base = (k // SUBLANES) * SUBLANES
            acc = acc + w_ref[k:k + 1, pl.ds(c0, LANES)] * zshift[k % SUBLANES, base:base + tt, :]
        ybuf[:, pl.ds(c0, LANES)] = acc
        return 0

    lax.fori_loop(0, ch // LANES, chan_block, 0)

    y = ybuf[...]
    mu = jnp.mean(y, axis=-1, keepdims=True)
    yc = y - mu
    var = jnp.mean(yc * yc, axis=-1, keepdims=True)
    zn = yc * lax.rsqrt(var + LN_EPS) * g_ref[...] + lb_ref[...]
    o_ref[0] = (zn * _sigmoid(zn)).astype(o_ref.dtype)
    zbuf[0:CONV_HALO, :] = zbuf[tt:tt + CONV_HALO, :]


def _conv_branch(z3, col_a, col_b, conv_w, conv_b, ln_g, ln_b):
    batch, lp, _ = z3.shape
    ch = conv_b.shape[1]
    tt = SEQ_TILE
    ab, bb = col_a // ch, col_b // ch
    return pl.pallas_call(
        _conv_kernel,
        grid=(batch, lp // tt),
        in_specs=[
            pl.BlockSpec((1, tt, ch), lambda b, t: (b, t, ab)),
            pl.BlockSpec((1, tt, ch), lambda b, t: (b, t, bb)),
            pl.BlockSpec(conv_w.shape, lambda b, t: (0, 0)),
            pl.BlockSpec((1, ch), lambda b, t: (0, 0)),
            pl.BlockSpec((1, ch), lambda b, t: (0, 0)),
            pl.BlockSpec((1, ch), lambda b, t: (0, 0)),
        ],
        out_specs=pl.BlockSpec((1, tt, ch), lambda b, t: (b, t, 0)),
        out_shape=jax.ShapeDtypeStruct((batch, lp, ch), BF16),
        scratch_shapes=[
            pltpu.VMEM((tt + CONV_HALO, ch), F32), pltpu.VMEM((tt, ch), F32),
            pltpu.VMEM((SUBLANES, tt + ((CONV_K - 1) // SUBLANES) * SUBLANES, LANES), F32),
        ],
        compiler_params=pltpu.CompilerParams(dimension_semantics=("parallel", "arbitrary")),
        name="conv_branch",
    )(z3, z3, conv_w, conv_b, ln_g, ln_b)


def _merge_route_kernel(att_ref, cv_ref, ga0_ref, ga1_ref, gc0_ref, gc1_ref, r_ref, wao_ref, wco_ref,
                        wo_ref, g_ref, wr_ref, br_ref, ro_ref, n_ref, route_ref, cnt_ref, cnt_sc,
                        rn_sc, *, tiles_per_seq):
    i = pl.program_id(0)
    tm = r_ref.shape[0]

    @pl.when(i == 0)
    def _():
        cnt_sc[...] = jnp.zeros_like(cnt_sc)
        rn_sc[...] = jnp.zeros_like(rn_sc)

    br_att = jnp.dot(att_ref[...], wao_ref[...], preferred_element_type=F32)

    r_prev = rn_sc[...]
    ms = jnp.mean(r_prev * r_prev, axis=-1, keepdims=True)
    n = r_prev * lax.rsqrt(ms + RMS_EPS) * g_ref[...]
    n_ref[...] = _pack_rows(n)
    logits = jnp.dot(n.astype(BF16), wr_ref[...], preferred_element_type=F32) + br_ref[...]

    br_conv = jnp.dot(cv_ref[...], wco_ref[...], preferred_element_type=F32)
    lane = lax.broadcasted_iota(I32, logits.shape, 1)
    big = jnp.int32(4 * LANES)
    first_e = ROUTE_FIRST_EXPERT_LANE

    gl = jnp.where(lane < N_GROUPS, logits, -jnp.inf)
    gmax = jnp.max(gl, axis=-1, keepdims=True)
    gsum = jnp.sum(jnp.exp(gl - gmax), axis=-1, keepdims=True)
    g_w = 1.0 / gsum
    g_idx = jnp.min(jnp.where(gl == gmax, lane, big), axis=-1, keepdims=True)

    in_group = (lane >= first_e + g_idx * EXPERTS_PER_GROUP) & \
               (lane < first_e + (g_idx + 1) * EXPERTS_PER_GROUP)
    el = jnp.where(in_group, logits, -jnp.inf)
    v0 = jnp.max(el, axis=-1, keepdims=True)
    i0 = jnp.min(jnp.where(el == v0, lane, big), axis=-1, keepdims=True)
    el = jnp.where(lane == i0, -jnp.inf, el)
    v1 = jnp.max(el, axis=-1, keepdims=True)
    i1 = jnp.min(jnp.where(el == v1, lane, big), axis=-1, keepdims=True)
    e1 = jnp.exp(v1 - v0)
    w0 = g_w / (1.0 + e1)
    w1 = g_w * e1 / (1.0 + e1)

    pos = ((i + tiles_per_seq - 1) % tiles_per_seq) * tm + lax.broadcasted_iota(I32, (tm, 1), 0)
    valid = (pos >= META0) & (i >= 1)
    onehot = jnp.where(((lane == i0) | (lane == i1)) & valid, 1.0, 0.0)
    rr = lax.broadcasted_iota(I32, (tm, tm), 0)
    cc = lax.broadcasted_iota(I32, (tm, tm), 1)
    lower = jnp.where(cc < rr, 1.0, 0.0).astype(BF16)
    rank = jnp.dot(lower, onehot.astype(BF16), preferred_element_type=F32) + cnt_sc[0:1, :]

    g_att = jnp.concatenate([ga0_ref[...], ga1_ref[...]], axis=1).astype(F32)
    g_conv = jnp.concatenate([gc0_ref[...], gc1_ref[...]], axis=1).astype(F32)
    merged = _sigmoid(g_att) * br_att + _sigmoid(g_conv) * br_conv
    r_new = r_ref[...] + jnp.dot(merged.astype(BF16), wo_ref[...], preferred_element_type=F32)
    ro_ref[...] = r_new
    rn_sc[...] = r_new

    rank0 = jnp.sum(jnp.where(lane == i0, rank, 0.0), axis=-1, keepdims=True)
    rank1 = jnp.sum(jnp.where(lane == i1, rank, 0.0), axis=-1, keepdims=True)
    cnt_sc[0:1, :] = cnt_sc[0:1, :] + jnp.sum(onehot, axis=0, keepdims=True)
    cnt_ref[...] = jnp.broadcast_to(cnt_sc[0:1, :], cnt_ref.shape)

    w0 = jnp.where(valid, w0, 0.0)
    w1 = jnp.where(valid, w1, 0.0)
    out = jnp.where(lane == 0, (i0 - first_e).astype(F32), 0.0)
    out = jnp.where(lane == 1, (i1 - first_e).astype(F32), out)
    out = jnp.where(lane == 2, w0, out)
    out = jnp.where(lane == 3, w1, out)
    out = jnp.where(lane == 4, rank0, out)
    out = jnp.where(lane == 5, rank1, out)
    route_ref[...] = out


def _merge_route(att, cv, z, col_ga, col_gc, r, wao, wco, wo, gain, wr, br, tiles_per_seq):
    n_tok, d = r.shape
    aw = att.shape[1]
    cw = cv.shape[1]
    tm = SEQ_TILE
    half = d // 2
    gab, gcb = col_ga // half, col_gc // half
    const = lambda i: (0, 0)
    single = pl.Buffered(1)
    vmem = (2 * tm * (aw * 2 + cw * 2 + 2 * d * 2 + 3 * d * 4 + LANES * 4)
            + (aw * d + cw * d + d * d + d * LANES) * 2 + 8 * tm * d * 4)
    n_tiles = n_tok // tm
    cur = lambda i: jnp.minimum(i, n_tiles - 1)
    prev = lambda i: jnp.maximum(i - 1, 0)
    return pl.pallas_call(
        functools.partial(_merge_route_kernel, tiles_per_seq=tiles_per_seq),
        grid=(n_tiles + 1,),
        in_specs=[
            pl.BlockSpec((tm, aw), lambda i: (cur(i), 0)),
            pl.BlockSpec((tm, cw), lambda i: (cur(i), 0)),
            pl.BlockSpec((tm, half), lambda i: (cur(i), gab)),
            pl.BlockSpec((tm, half), lambda i: (cur(i), gab + 1)),
            pl.BlockSpec((tm, half), lambda i: (cur(i), gcb)),
            pl.BlockSpec((tm, half), lambda i: (cur(i), gcb + 1)),
            pl.BlockSpec((tm, d), lambda i: (cur(i), 0)),
            pl.BlockSpec(wao.shape, const, pipeline_mode=single),
            pl.BlockSpec(wco.shape, const, pipeline_mode=single),
            pl.BlockSpec(wo.shape, const, pipeline_mode=single),
            pl.BlockSpec((1, d), const),
            pl.BlockSpec(wr.shape, const, pipeline_mode=single),
            pl.BlockSpec((1, LANES), const),
        ],
        out_specs=[
            pl.BlockSpec((tm, d), lambda i: (cur(i), 0)),
            pl.BlockSpec((tm, d // 2), lambda i: (prev(i), 0)),
            pl.BlockSpec((tm, LANES), lambda i: (prev(i), 0)),
            pl.BlockSpec((8, LANES), const),
        ],
        out_shape=[
            jax.ShapeDtypeStruct((n_tok, d), F32),
            jax.ShapeDtypeStruct((n_tok, d // 2), U32),
            jax.ShapeDtypeStruct((n_tok, LANES), F32),
            jax.ShapeDtypeStruct((8, LANES), F32),
        ],
        scratch_shapes=[pltpu.VMEM((8, LANES), F32), pltpu.VMEM((tm, d), F32)],
        compiler_params=pltpu.CompilerParams(
            dimension_semantics=("arbitrary",), vmem_limit_bytes=_vmem_limit(vmem + tm * d * 4)),
        name="merge_route",
    )(att, cv, z, z, z, z, r, wao, wco, wo, gain, wr, br)


def _dispatch_kernel(dest_ref, zflag_ref, n_ref, xs_ref, zero_sc, stage, sem, zsem, *, tiles_per_seq):
    i = pl.program_id(0)
    tm = zero_sc.shape[0]
    n_tiles = xs_ref.shape[0] // tm

    def zero_copy(t):
        return pltpu.make_async_copy(zero_sc, xs_ref.at[pl.ds(pl.multiple_of(t * tm, tm), tm)], zsem)

    @pl.when(i == 0)
    def _():
        zero_sc[...] = jnp.zeros_like(zero_sc)

        def issue_zero(t, _):
            @pl.when(zflag_ref[t] != 0)
            def _():
                zero_copy(t).start()
            return 0

        def drain_zero(t, _):
            @pl.when(zflag_ref[t] != 0)
            def _():
                zero_copy(t).wait()
            return 0

        lax.fori_loop(0, n_tiles, issue_zero, 0)
        lax.fori_loop(0, n_tiles, drain_zero, 0)

    slot = i % 2
    last = pl.num_programs(0) - 1

    def first_row_group(step):
        return jnp.where(step % tiles_per_seq == 0, META0, 0) // SUBLANES

    def row_copy(step, buf, g, u, k):
        d = dest_ref[(step * tm + g * SUBLANES + u) * 2 + k]
        return pltpu.make_async_copy(stage.at[buf, g, pl.ds(u, 1)], xs_ref.at[pl.ds(d, 1)], sem.at[buf])

    def drain(step, buf):
        one_row = pltpu.make_async_copy(stage.at[buf, 0, pl.ds(0, 1)], xs_ref.at[pl.ds(0, 1)], sem.at[buf])

        def body(g, _):
            for _u in range(2 * SUBLANES):
                one_row.wait()
            return 0
        lax.fori_loop(first_row_group(step), tm // SUBLANES, body, 0)

    @pl.when(i >= 2)
    def _():
        drain(i - 2, slot)

    stage[slot] = n_ref[...]

    def issue(g, _):
        for u in range(SUBLANES):
            row_copy(i, slot, g, u, 0).start()
            row_copy(i, slot, g, u, 1).start()
        return 0

    lax.fori_loop(first_row_group(i), tm // SUBLANES, issue, 0)

    @pl.when(i == last)
    def _():
        @pl.when(i >= 1)
        def _():
            drain(i - 1, 1 - slot)
        drain(i, slot)


def _dispatch(dest, zflag, n, n_rows, tiles_per_seq):
    n_tok, w = n.shape
    tm = MOE_TM
    groups = tm // SUBLANES
    return pl.pallas_call(
        functools.partial(_dispatch_kernel, tiles_per_seq=tiles_per_seq),
        grid_spec=pltpu.PrefetchScalarGridSpec(
            num_scalar_prefetch=2,
            grid=(n_tok // tm,),
            in_specs=[pl.BlockSpec((groups, SUBLANES, w), lambda i, *_: (i, 0, 0))],
            out_specs=pl.BlockSpec(memory_space=pl.ANY),
            scratch_shapes=[pltpu.VMEM((tm, w), n.dtype), pltpu.VMEM((2, groups, SUBLANES, w), n.dtype),
                            pltpu.SemaphoreType.DMA((2,)), pltpu.SemaphoreType.DMA],
        ),
        out_shape=jax.ShapeDtypeStruct((n_rows, w), n.dtype),
        compiler_params=pltpu.CompilerParams(dimension_semantics=("arbitrary",),
                                             has_side_effects=True),
        name="moe_dispatch",
    )(dest, zflag, n.reshape(n_tok // SUBLANES, SUBLANES, w))


def _expert_kernel(te_ref, first_ref, nxt_ref, na_ref, x_ref, wgu_hbm, wdn_hbm, y_ref,
                   gu_stage, dn_stage, gu_bf, dn_bf, sem, *, layer):
    i = pl.program_id(0)
    active = i < na_ref[0]

    def weight_copies(e):
        return (pltpu.make_async_copy(wgu_hbm.at[layer, e], gu_stage, sem.at[0]),
                pltpu.make_async_copy(wdn_hbm.at[layer, e], dn_stage, sem.at[1]))

    @pl.when(i == 0)
    def _():
        for cp in weight_copies(te_ref[0]):
            cp.start()

    @pl.when(active & (first_ref[i] != 0))
    def _():
        for cp in weight_copies(te_ref[i]):
            cp.wait()
        gu_bf[...] = gu_stage[...].astype(BF16)
        dn_bf[...] = dn_stage[...].astype(BF16)

        @pl.when(nxt_ref[i] >= 0)
        def _():
            for cp in weight_copies(nxt_ref[i]):
                cp.start()

    @pl.when(active)
    def _():
        de = dn_bf.shape[0]
        x = _unpack_rows(x_ref[...]).astype(BF16)
        h = jnp.dot(x, gu_bf[...], preferred_element_type=F32)
        a = h[:, :de]
        b = h[:, de:]
        act = a * _sigmoid(a) * b
        y_ref[...] = _pack_rows(jnp.dot(act.astype(BF16), dn_bf[...], preferred_element_type=F32))

    @pl.when(jnp.logical_not(active))
    def _():
        y_ref[...] = jnp.zeros_like(y_ref)


def _experts(tile_e, first, nxt, n_act, xs, w_gate_up, w_down, layer):
    n_rows, w = xs.shape
    d = w_gate_up.shape[2]
    assert d == 2 * w
    tm = MOE_TM
    de2 = w_gate_up.shape[3]
    de = w_down.shape[2]
    row_map = lambda i, te, fi, nx, na: (jnp.minimum(i, na[0] - 1), 0)
    vmem = (d * de2 + de * d) * (4 + 2) + 6 * tm * d * 4 + 6 * tm * de2 * 4
    return pl.pallas_call(
        functools.partial(_expert_kernel, layer=layer),
        grid_spec=pltpu.PrefetchScalarGridSpec(
            num_scalar_prefetch=4,
            grid=(n_rows // tm,),
            in_specs=[
                pl.BlockSpec((tm, w), row_map),
                pl.BlockSpec(memory_space=pl.ANY),
                pl.BlockSpec(memory_space=pl.ANY),
            ],
            out_specs=pl.BlockSpec((tm, w), lambda i, *_: (i, 0)),
            scratch_shapes=[
                pltpu.VMEM((d, de2), F32), pltpu.VMEM((de, d), F32),
                pltpu.VMEM((d, de2), BF16), pltpu.VMEM((de, d), BF16),
                pltpu.SemaphoreType.DMA((2,)),
            ],
        ),
        out_shape=jax.ShapeDtypeStruct((n_rows, w), xs.dtype),
        compiler_params=pltpu.CompilerParams(
            dimension_semantics=("arbitrary",), vmem_limit_bytes=_vmem_limit(vmem)),
        name="moe_experts",
    )(tile_e, first, nxt, n_act, xs, w_gate_up, w_down)


def _combine_kernel(dest_ref, r_ref, route_ref, y_ref, *rest, tile_of, final):
    if final:
        g_ref, o_ref, ybuf, sem = rest
    else:
        o_ref, ybuf, sem = rest
    tm = r_ref.shape[0]
    i = pl.program_id(0)
    slot = i % 2

    def issue(step, buf):
        tile = tile_of(step)

        def body(g, _):
            for u in range(SUBLANES):
                for k in range(2):
                    d = dest_ref[(tile * tm + g * SUBLANES + u) * 2 + k]
                    pltpu.make_async_copy(y_ref.at[pl.ds(d, 1)], ybuf.at[buf, k, g, pl.ds(u, 1)],
                                          sem.at[buf]).start()
            return 0
        lax.fori_loop(0, tm // SUBLANES, body, 0)

    @pl.when(i == 0)
    def _():
        issue(i, slot)

    @pl.when(i + 1 < pl.num_programs(0))
    def _():
        issue(i + 1, 1 - slot)

    one_row = pltpu.make_async_copy(y_ref.at[pl.ds(0, 1)], ybuf.at[slot, 0, 0, pl.ds(0, 1)], sem.at[slot])

    def drain(g, _):
        for _u in range(2 * SUBLANES):
            one_row.wait()
        return 0
    lax.fori_loop(0, tm // SUBLANES, drain, 0)

    route = route_ref[...]
    w = ybuf.shape[-1]
    out = (r_ref[...] + route[:, 2:3] * _unpack_rows(ybuf[slot, 0].reshape(tm, w))
           + route[:, 3:4] * _unpack_rows(ybuf[slot, 1].reshape(tm, w)))
    if final:
        ms = jnp.mean(out * out, axis=-1, keepdims=True)
        out = out * lax.rsqrt(ms + RMS_EPS) * g_ref[...]
    o_ref[...] = out


def _combine(dest, r, route, y, tiles_per_seq, final_gain=None):
    n_tok, d = r.shape
    tm = MOE_TM
    final = final_gain is not None
    if final:
        real_tiles = tiles_per_seq - P0 // tm
        tile_of = lambda i: (i // real_tiles) * tiles_per_seq + P0 // tm + i % real_tiles
        n_out_tiles = (n_tok // tm // tiles_per_seq) * real_tiles
    else:
        tile_of = lambda i: i
        n_out_tiles = n_tok // tm
    in_specs = [
        pl.BlockSpec((tm, d), lambda i, *_: (tile_of(i), 0)),
        pl.BlockSpec((tm, LANES), lambda i, *_: (tile_of(i), 0)),
        pl.BlockSpec(memory_space=pl.ANY),
    ]
    args = [dest, r, route, y]
    if final:
        in_specs.append(pl.BlockSpec((1, d), lambda i, *_: (0, 0)))
        args.append(final_gain)
    return pl.pallas_call(
        functools.partial(_combine_kernel, tile_of=tile_of, final=final),
        grid_spec=pltpu.PrefetchScalarGridSpec(
            num_scalar_prefetch=1,
            grid=(n_out_tiles,),
            in_specs=in_specs,
            out_specs=pl.BlockSpec((tm, d), lambda i, *_: (i, 0)),
            scratch_shapes=[pltpu.VMEM((2, 2, tm // SUBLANES, SUBLANES, d // 2), y.dtype),
                            pltpu.SemaphoreType.DMA((2,))],
        ),
        out_shape=jax.ShapeDtypeStruct((n_out_tiles * tm, d), F32),
        compiler_params=pltpu.CompilerParams(
            dimension_semantics=("arbitrary",),
            vmem_limit_bytes=_vmem_limit(8 * tm * d * 4 + 4 * tm * d * 4 + (8 << 20))),
        name="moe_combine_final" if final else "moe_combine",
    )(*args)


def _routing_tables(route, counts, n_tiles_max, valid_tok):
    tm = MOE_TM
    cnt = counts[0, ROUTE_FIRST_EXPERT_LANE:ROUTE_FIRST_EXPERT_LANE + N_EXPERTS].astype(I32)
    ntile = (cnt + tm - 1) // tm
    tile_end = jnp.cumsum(ntile)
    tile_start = tile_end - ntile
    n_act = tile_end[-1:]
    e_ids = route[:, 0:2].astype(I32)
    rank = route[:, 4:6].astype(I32)
    dest = jnp.where(valid_tok[:, None], tile_start[e_ids] * tm + rank, 0).reshape(-1)
    all_tiles = jnp.arange(n_tiles_max, dtype=I32)
    clamped = jnp.minimum(all_tiles, n_act[0] - 1)
    tile_e = jnp.sum((clamped[:, None] >= tile_end[None, :]).astype(I32), axis=1)
    tile_e = jnp.minimum(tile_e, N_EXPERTS - 1)
    first = jnp.concatenate([jnp.ones((1,), I32), (tile_e[1:] != tile_e[:-1]).astype(I32)])
    end_of_mine = tile_end[tile_e]
    nxt = jnp.where(end_of_mine < n_act[0], tile_e[jnp.minimum(end_of_mine, n_tiles_max - 1)], -1)
    is_expert_tail = jnp.any((all_tiles[:, None] == tile_end[None, :] - 1) & (ntile[None, :] > 0), axis=1)
    zflag = (is_expert_tail | (all_tiles >= n_act[0])).astype(I32)
    return dest, tile_e, first, nxt.astype(I32), n_act.astype(I32), zflag


def kernel(x, meta, norm_mix, w_in, b_forget, w_attn_out, conv_w, conv_b, conv_ln_g, conv_ln_b,
           w_conv_out, w_out, norm_ffn, w_router_group, b_router_group, w_router_expert,
           b_router_expert, w_gate_up, w_down, norm_final):
    batch, seq, d = x.shape
    depth = w_in.shape[0]
    heads = b_forget.shape[1]
    att_w = heads * HEAD_DIM
    conv_c = conv_b.shape[1]
    lp = P0 + seq
    tiles_per_seq = lp // SEQ_TILE
    n_tok = batch * lp
    assert seq % SEQ_TILE == 0 and n_tok % INPROJ_TM == 0
    assert (3 * att_w) % INPROJ_TN == 0 and (2 * conv_c + 2 * d) % INPROJ_TN == 0 and d % 2 == 0
    assert att_w % LANES == 0 and conv_c % LANES == 0 and SEQ_TILE == MOE_TM

    r = jnp.pad(x, ((0, 0), (P0, 0), (0, 0)))
    r = lax.dynamic_update_slice(
        r, jnp.broadcast_to(meta.astype(x.dtype)[None], (batch, N_META, d)), (0, META0, 0))
    r = r.reshape(n_tok, d)

    c_f = 3 * att_w
    c_u = c_f + heads
    m_q, m_k, m_v = 0, att_w, 2 * att_w
    m_ua = 3 * att_w
    m_ub = m_ua + conv_c
    m_ga = m_ub + conv_c
    m_gc = m_ga + d

    valid_tok = jnp.tile(jnp.arange(lp) >= META0, batch)
    n_pairs = 2 * batch * (lp - META0)
    n_tiles_max = n_pairs // MOE_TM + N_EXPERTS
    n_rows = n_tiles_max * MOE_TM

    out = None
    for l in range(depth):
        wf = jnp.pad(w_in[l, :, c_f:c_u], ((0, 0), (0, LANES - heads))).astype(BF16)
        z, f = _inproj(r, norm_mix[l][None], w_in[l, :, :c_f].astype(BF16),
                       w_in[l, :, c_u:].astype(BF16), wf)
        cum = _forget_cumsum(f.reshape(batch, lp, LANES),
                             jnp.pad(b_forget[l], (0, LANES - heads))[None])
        z3 = z.reshape(batch, lp, -1)
        att = _attention(z3, cum, m_q, m_k, m_v).reshape(n_tok, att_w)
        conv_w_pad = jnp.concatenate([conv_w[l], jnp.zeros((1, conv_c), F32)], axis=0)
        cv = _conv_branch(z3, m_ua, m_ub, conv_w_pad, conv_b[l][None], conv_ln_g[l][None],
                          conv_ln_b[l][None]).reshape(n_tok, conv_c)

        wr = jnp.zeros((d, LANES), F32)
        wr = wr.at[:, :N_GROUPS].set(w_router_group[l])
        wr = wr.at[:, N_GROUPS:N_GROUPS + N_EXPERTS].set(w_router_expert[l]).astype(BF16)
        br = jnp.zeros((1, LANES), F32)
        br = br.at[0, :N_GROUPS].set(b_router_group[l])
        br = br.at[0, N_GROUPS:N_GROUPS + N_EXPERTS].set(b_router_expert[l])
        r, n2, route, counts = _merge_route(
            att, cv, z, m_ga, m_gc, r, w_attn_out[l].astype(BF16), w_conv_out[l].astype(BF16),
            w_out[l].astype(BF16), norm_ffn[l][None], wr, br, tiles_per_seq)

        dest, tile_e, first, nxt, n_act, zflag = _routing_tables(route, counts, n_tiles_max, valid_tok)
        xs = _dispatch(dest, zflag, n2, n_rows, tiles_per_seq)
        y = _experts(tile_e, first, nxt, n_act, xs, w_gate_up, w_down, l)
        if l + 1 < depth:
            r = _combine(dest, r, route, y, tiles_per_seq)
        else:
            out = _combine(dest, r, route, y, tiles_per_seq, final_gain=norm_final[None])
    return out.reshape(batch, seq, d)
```

```python
import functools

import jax
import jax.numpy as jnp
from jax import lax
from jax.experimental import pallas as pl
from jax.experimental.pallas import tpu as pltpu

F32 = jnp.float32
BF16 = jnp.bfloat16
I32 = jnp.int32
U32 = jnp.uint32

LANES = 128
SUBLANES = 8
MXU_DIM = 256
VMEM_BYTES_V7X = 64 * 1024 * 1024

N_META = 16
HEAD_DIM = 64
HEADS_PER_BLOCK = LANES // HEAD_DIM
ATT_HEADS = 8
ATT_BLOCK = ATT_HEADS * HEAD_DIM
CONV_K = 31
N_GROUPS = 4
EXPERTS_PER_GROUP = 8
N_EXPERTS = N_GROUPS * EXPERTS_PER_GROUP
LOG2_E = 1.4426950408889634
RMS_EPS = 1e-6
LN_EPS = 1e-5

SEQ_TILE = 256
P0 = SEQ_TILE
META0 = P0 - N_META
INPROJ_TM = 1024
INPROJ_TN = 1024
MOE_TM = 256
CONV_HALO = 32
DMA_UNROLL = 8
ROUTE_FIRST_EXPERT_LANE = N_GROUPS
ROUTE_FIELDS = 8
MASK_VALUE = -1e30


def _vmem_limit(nbytes):
    return int(min(max(nbytes, 16 * 1024 * 1024), VMEM_BYTES_V7X - 6 * 1024 * 1024))


def _sigmoid(x):
    return 1.0 / (1.0 + jnp.exp(-x))


def _pack_rows(v):
    c = v.shape[1] // 2
    lo = lax.bitcast_convert_type(v[:, :c].astype(BF16).astype(F32), U32)
    hi = lax.bitcast_convert_type(v[:, c:].astype(BF16).astype(F32), U32)
    return (hi & jnp.uint32(0xFFFF0000)) | (lo >> 16)


def _unpack_rows(w):
    lo = lax.bitcast_convert_type(w << 16, F32)
    hi = lax.bitcast_convert_type(w & jnp.uint32(0xFFFF0000), F32)
    return jnp.concatenate([lo, hi], axis=1)


def _inproj_kernel(r_ref, g_ref, wa_ref, wb_ref, wf_ref, z_ref, f_ref, n_sc, *, na):
    j = pl.program_id(1)

    @pl.when(j == 0)
    def _():
        x = r_ref[...]
        ms = jnp.mean(x * x, axis=-1, keepdims=True)
        n = (x * lax.rsqrt(ms + RMS_EPS) * g_ref[...]).astype(BF16)
        n_sc[...] = n
        f_ref[...] = jnp.dot(n, wf_ref[...], preferred_element_type=F32)

    @pl.when(j < na)
    def _():
        z_ref[...] = jnp.dot(n_sc[...], wa_ref[...], preferred_element_type=F32).astype(z_ref.dtype)

    @pl.when(j >= na)
    def _():
        z_ref[...] = jnp.dot(n_sc[...], wb_ref[...], preferred_element_type=F32).astype(z_ref.dtype)


def _inproj(r, gain, w_a, w_b, wf):
    n_tok, d = r.shape
    tm, tn = INPROJ_TM, INPROJ_TN
    na, nb = w_a.shape[1] // tn, w_b.shape[1] // tn
    cols = (na + nb) * tn
    vmem = 2 * tm * d * 4 + 4 * d * tn * 2 + 2 * tm * tn * 2 + tm * d * 2 + 2 * (d * 2 + tm * 4) * LANES
    return pl.pallas_call(
        functools.partial(_inproj_kernel, na=na),
        grid=(n_tok // tm, na + nb),
        in_specs=[
            pl.BlockSpec((tm, d), lambda i, j: (i, 0)),
            pl.BlockSpec((1, d), lambda i, j: (0, 0)),
            pl.BlockSpec((d, tn), lambda i, j: (0, jnp.minimum(j, na - 1))),
            pl.BlockSpec((d, tn), lambda i, j: (0, jnp.maximum(j - na, 0))),
            pl.BlockSpec((d, LANES), lambda i, j: (0, 0)),
        ],
        out_specs=[
            pl.BlockSpec((tm, tn), lambda i, j: (i, j)),
            pl.BlockSpec((tm, LANES), lambda i, j: (i, 0)),
        ],
        out_shape=[
            jax.ShapeDtypeStruct((n_tok, cols), BF16),
            jax.ShapeDtypeStruct((n_tok, LANES), F32),
        ],
        scratch_shapes=[pltpu.VMEM((tm, d), BF16)],
        compiler_params=pltpu.CompilerParams(
            dimension_semantics=("parallel", "arbitrary"),
            vmem_limit_bytes=_vmem_limit(vmem + (8 << 20))),
        name="inproj",
    )(r, gain, w_a, w_b, wf)


def _cum_kernel(f_ref, b_ref, o_ref):
    ch = MXU_DIM
    lp = f_ref.shape[1]
    row = lax.broadcasted_iota(I32, (ch, ch), 0)
    col = lax.broadcasted_iota(I32, (ch, ch), 1)
    lower = (col <= row).astype(F32)
    carry = jnp.zeros((1, LANES), F32)
    for c in range(lp // ch):
        x = f_ref[0, c * ch:(c + 1) * ch, :] + b_ref[...]
        log_f = jnp.minimum(x, 0.0) - jnp.log1p(jnp.exp(-jnp.abs(x)))
        loc = jnp.dot(lower, log_f, preferred_element_type=F32,
                      precision=lax.Precision.HIGHEST) + carry
        o_ref[0, c * ch:(c + 1) * ch, :] = loc
        carry = loc[ch - 1:ch, :]


def _forget_cumsum(f3, b_forget):
    batch, lp, _ = f3.shape
    return pl.pallas_call(
        _cum_kernel,
        grid=(batch,),
        in_specs=[
            pl.BlockSpec((1, lp, LANES), lambda b: (b, 0, 0)),
            pl.BlockSpec((1, LANES), lambda b: (0, 0)),
        ],
        out_specs=pl.BlockSpec((1, lp, LANES), lambda b: (b, 0, 0)),
        out_shape=jax.ShapeDtypeStruct((batch, lp, LANES), F32),
        compiler_params=pltpu.CompilerParams(dimension_semantics=("parallel",)),
        name="forget_cumsum",
    )(f3, b_forget)


def _attn_kernel(q_ref, k_ref, v_ref, c_ref, o_ref, kaug_sc, vt_sc, qaug_sc, s_a, s_b,
                 m_sc, l_sc, acc_sc):
    tq = q_ref.shape[1]
    tk = tq
    lp = k_ref.shape[1]
    hg = pl.program_id(1)
    qi = pl.program_id(2)
    lane = lax.broadcasted_iota(I32, (1, LANES), 1)
    nt_dims = (((1,), (1,)), ((), ()))

    def own_lanes(h):
        return lane < HEAD_DIM if h % HEADS_PER_BLOCK == 0 else lane >= HEAD_DIM

    def bias_lane(h):
        return HEAD_DIM if h % HEADS_PER_BLOCK == 0 else 0

    def block_lanes(h):
        blk = h // HEADS_PER_BLOCK
        return slice(blk * LANES, (blk + 1) * LANES)

    @pl.when(qi == 0)
    def _():
        for c in range(lp // tk):
            rows = slice(c * tk, (c + 1) * tk)
            vt_sc[:, rows] = v_ref[0, rows, :].astype(F32).T.astype(BF16)
            cum = c_ref[0, rows, :]
            key_pos = c * tk + lax.broadcasted_iota(I32, (tk, 1), 0)
            for h in range(ATT_HEADS):
                head = hg * ATT_HEADS + h
                col = jnp.sum(jnp.where(lane == head, cum, 0.0), axis=-1, keepdims=True) * LOG2_E
                hi = col.astype(BF16).astype(F32)
                rem = col - hi
                mid = rem.astype(BF16).astype(F32)
                low = rem - mid
                hi = jnp.where(key_pos < META0, -MASK_VALUE, hi)
                a = bias_lane(h)
                bias = jnp.where(lane == a, hi, jnp.where(lane == a + 1, mid,
                                                          jnp.where(lane == a + 2, low, 0.0)))
                kaug_sc[h, rows, :] = jnp.where(own_lanes(h), k_ref[0, rows, block_lanes(h)],
                                                bias.astype(BF16))

    for h in range(ATT_HEADS):
        a = bias_lane(h)
        minus_one = jnp.where((lane >= a) & (lane < a + 3), -1.0, 0.0)
        scaled = q_ref[0, :, block_lanes(h)].astype(F32) * (HEAD_DIM ** -0.5 * LOG2_E)
        qaug_sc[h] = jnp.where(own_lanes(h), scaled, minus_one).astype(BF16)
        m_sc[h] = jnp.full((1, tq), MASK_VALUE, F32)
        l_sc[h] = jnp.zeros((1, tq), F32)
        acc_sc[h] = jnp.zeros((HEAD_DIM, tq), F32)

    def scores_into(dst, j):
        s0 = pl.multiple_of(j * tk, tk)
        for h in range(ATT_HEADS):
            dst[h] = lax.dot_general(kaug_sc[h, pl.ds(s0, tk), :], qaug_sc[h], nt_dims,
                                     preferred_element_type=F32)

    def softmax_pv(src, j, diagonal):
        s0 = pl.multiple_of(j * tk, tk)
        for h in range(ATT_HEADS):
            st = src[h]
            if diagonal:
                key = lax.broadcasted_iota(I32, (tk, tq), 0)
                qry = lax.broadcasted_iota(I32, (tk, tq), 1)
                st = jnp.where(key <= qry, st, MASK_VALUE)
            m = m_sc[h]
            m_new = jnp.maximum(m, jnp.max(st, axis=0, keepdims=True))
            alpha = jnp.exp2(m - m_new)
            p = jnp.exp2(st - m_new)
            l_sc[h] = alpha * l_sc[h] + jnp.sum(p, axis=0, keepdims=True)
            pv = jnp.dot(vt_sc[h * HEAD_DIM:(h + 1) * HEAD_DIM, pl.ds(s0, tk)], p.astype(BF16),
                         preferred_element_type=F32)
            acc_sc[h] = alpha * acc_sc[h] + pv
            m_sc[h] = m_new

    scores_into(s_a, 0)

    def pair(i, _):
        j = 2 * i
        scores_into(s_b, j + 1)
        softmax_pv(s_a, j, False)
        scores_into(s_a, j + 2)
        softmax_pv(s_b, j + 1, False)
        return 0

    lax.fori_loop(0, qi // 2, pair, 0)

    @pl.when(qi % 2 == 0)
    def _():
        softmax_pv(s_a, qi, True)

    @pl.when(qi % 2 == 1)
    def _():
        scores_into(s_b, qi)
        softmax_pv(s_a, qi - 1, False)
        softmax_pv(s_b, qi, True)

    att_t = jnp.concatenate([acc_sc[h] / l_sc[h] for h in range(ATT_HEADS)], axis=0)
    o_ref[0] = att_t.T.astype(o_ref.dtype)


def _attention(z3, cum, col_q, col_k, col_v):
    batch, lp, _ = z3.shape
    att_w = col_k - col_q
    tq = SEQ_TILE
    qb, kb, vb = col_q // ATT_BLOCK, col_k // ATT_BLOCK, col_v // ATT_BLOCK
    return pl.pallas_call(
        _attn_kernel,
        grid=(batch, att_w // ATT_BLOCK, lp // tq),
        in_specs=[
            pl.BlockSpec((1, tq, ATT_BLOCK), lambda b, h, i: (b, i, qb + h)),
            pl.BlockSpec((1, lp, ATT_BLOCK), lambda b, h, i: (b, 0, kb + h)),
            pl.BlockSpec((1, lp, ATT_BLOCK), lambda b, h, i: (b, 0, vb + h)),
            pl.BlockSpec((1, lp, LANES), lambda b, h, i: (b, 0, 0)),
        ],
        out_specs=pl.BlockSpec((1, tq, ATT_BLOCK), lambda b, h, i: (b, i, h)),
        out_shape=jax.ShapeDtypeStruct((batch, lp, att_w), BF16),
        scratch_shapes=[
            pltpu.VMEM((ATT_HEADS, lp, LANES), BF16), pltpu.VMEM((ATT_BLOCK, lp), BF16),
            pltpu.VMEM((ATT_HEADS, tq, LANES), BF16),
            pltpu.VMEM((ATT_HEADS, tq, tq), F32), pltpu.VMEM((ATT_HEADS, tq, tq), F32),
            pltpu.VMEM((ATT_HEADS, 1, tq), F32), pltpu.VMEM((ATT_HEADS, 1, tq), F32),
            pltpu.VMEM((ATT_HEADS, HEAD_DIM, tq), F32),
        ],
        compiler_params=pltpu.CompilerParams(
            dimension_semantics=("parallel", "parallel", "arbitrary")),
        name="fox_attention",
    )(z3, z3, z3, cum)


def _conv_kernel(a_ref, b_ref, w_ref, cb_ref, g_ref, lb_ref, o_ref, zbuf, ybuf, zshift):
    tt = a_ref.shape[1]
    ch = a_ref.shape[2]
    t = pl.program_id(1)

    @pl.when(t == 0)
    def _():
        zbuf[0:CONV_HALO, :] = jnp.zeros((CONV_HALO, ch), F32)

    z = a_ref[0].astype(F32) * _sigmoid(b_ref[0].astype(F32))
    rows = t * tt + lax.broadcasted_iota(I32, (tt, 1), 0)
    zbuf[CONV_HALO:CONV_HALO + tt, :] = jnp.where(rows >= META0, z, 0.0)

    first = CONV_HALO - (CONV_K - 1)

    def chan_block(cb, _):
        c0 = pl.multiple_of(cb * LANES, LANES)
        for res in range(SUBLANES):
            span = tt + ((CONV_K - 1 - res) // SUBLANES) * SUBLANES
            zshift[res, 0:span, :] = zbuf[first + res:first + res + span, pl.ds(c0, LANES)]
        acc = jnp.zeros((tt, LANES), F32) + cb_ref[:, pl.ds(c0, LANES)]
        for k in range(CONV_K):
            base = (k // SUBLANES) * SUBLANES
            acc = acc + w_ref[k:k + 1, pl.ds(c0, LANES)] * zshift[k % SUBLANES, base:base + tt, :]
        ybuf[:, pl.ds(c0, LANES)] = acc
        return 0

    lax.fori_loop(0, ch // LANES, chan_block, 0)

    y = ybuf[...]
    mu = jnp.mean(y, axis=-1, keepdims=True)
    yc = y - mu
    var = jnp.mean(yc * yc, axis=-1, keepdims=True)
    zn = yc * lax.rsqrt(var + LN_EPS) * g_ref[...] + lb_ref[...]
    o_ref[0] = (zn * _sigmoid(zn)).astype(o_ref.dtype)
    zbuf[0:CONV_HALO, :] = zbuf[tt:tt + CONV_HALO, :]


def _conv_branch(z3, col_a, col_b, conv_w, conv_b, ln_g, ln_b):
    batch, lp, _ = z3.shape
    ch = conv_b.shape[1]
    tt = SEQ_TILE
    ab, bb = col_a // ch, col_b // ch
    return pl.pallas_call(
        _conv_kernel,
        grid=(batch, lp // tt),
        in_specs=[
            pl.BlockSpec((1, tt, ch), lambda b, t: (b, t, ab)),
            pl.BlockSpec((1, tt, ch), lambda b, t: (b, t, bb)),
            pl.BlockSpec(conv_w.shape, lambda b, t: (0, 0)),
            pl.BlockSpec((1, ch), lambda b, t: (0, 0)),
            pl.BlockSpec((1, ch), lambda b, t: (0, 0)),
            pl.BlockSpec((1, ch), lambda b, t: (0, 0)),
        ],
        out_specs=pl.BlockSpec((1, tt, ch), lambda b, t: (b, t, 0)),
        out_shape=jax.ShapeDtypeStruct((batch, lp, ch), BF16),
        scratch_shapes=[
            pltpu.VMEM((tt + CONV_HALO, ch), F32), pltpu.VMEM((tt, ch), F32),
            pltpu.VMEM((SUBLANES, tt + ((CONV_K - 1) // SUBLANES) * SUBLANES, LANES), F32),
        ],
        compiler_params=pltpu.CompilerParams(dimension_semantics=("parallel", "arbitrary")),
        name="conv_branch",
    )(z3, z3, conv_w, conv_b, ln_g, ln_b)


def _merge_route_kernel(att_ref, cv_ref, ga0_ref, ga1_ref, gc0_ref, gc1_ref, r_ref, wao_ref, wco_ref,
                        wo_ref, g_ref, wr_ref, br_ref, ro_ref, n_ref, route_ref, route_t_ref, cnt_ref,
                        cnt_sc, rn_sc, *, tiles_per_seq):
    i = pl.program_id(0)
    tm = r_ref.shape[0]

    @pl.when(i == 0)
    def _():
        cnt_sc[...] = jnp.zeros_like(cnt_sc)
        rn_sc[...] = jnp.zeros_like(rn_sc)

    br_att = jnp.dot(att_ref[...], wao_ref[...], preferred_element_type=F32)

    r_prev = rn_sc[...]
    ms = jnp.mean(r_prev * r_prev, axis=-1, keepdims=True)
    n = r_prev * lax.rsqrt(ms + RMS_EPS) * g_ref[...]
    n_ref[...] = _pack_rows(n)
    logits = jnp.dot(n.astype(BF16), wr_ref[...], preferred_element_type=F32) + br_ref[...]

    br_conv = jnp.dot(cv_ref[...], wco_ref[...], preferred_element_type=F32)
    lane = lax.broadcasted_iota(I32, logits.shape, 1)
    big = jnp.int32(4 * LANES)
    first_e = ROUTE_FIRST_EXPERT_LANE

    gl = jnp.where(lane < N_GROUPS, logits, -jnp.inf)
    gmax = jnp.max(gl, axis=-1, keepdims=True)
    gsum = jnp.sum(jnp.exp(gl - gmax), axis=-1, keepdims=True)
    g_w = 1.0 / gsum
    g_idx = jnp.min(jnp.where(gl == gmax, lane, big), axis=-1, keepdims=True)

    in_group = (lane >= first_e + g_idx * EXPERTS_PER_GROUP) & \
               (lane < first_e + (g_idx + 1) * EXPERTS_PER_GROUP)
    el = jnp.where(in_group, logits, -jnp.inf)
    v0 = jnp.max(el, axis=-1, keepdims=True)
    i0 = jnp.min(jnp.where(el == v0, lane, big), axis=-1, keepdims=True)
    el = jnp.where(lane == i0, -jnp.inf, el)
    v1 = jnp.max(el, axis=-1, keepdims=True)
    i1 = jnp.min(jnp.where(el == v1, lane, big), axis=-1, keepdims=True)
    e1 = jnp.exp(v1 - v0)
    w0 = g_w / (1.0 + e1)
    w1 = g_w * e1 / (1.0 + e1)

    pos = ((i + tiles_per_seq - 1) % tiles_per_seq) * tm + lax.broadcasted_iota(I32, (tm, 1), 0)
    valid = (pos >= META0) & (i >= 1)
    onehot = jnp.where(((lane == i0) | (lane == i1)) & valid, 1.0, 0.0)
    rr = lax.broadcasted_iota(I32, (tm, tm), 0)
    cc = lax.broadcasted_iota(I32, (tm, tm), 1)
    lower = jnp.where(cc < rr, 1.0, 0.0).astype(BF16)
    rank = jnp.dot(lower, onehot.astype(BF16), preferred_element_type=F32) + cnt_sc[0:1, :]

    g_att = jnp.concatenate([ga0_ref[...], ga1_ref[...]], axis=1).astype(F32)
    g_conv = jnp.concatenate([gc0_ref[...], gc1_ref[...]], axis=1).astype(F32)
    merged = _sigmoid(g_att) * br_att + _sigmoid(g_conv) * br_conv
    r_new = r_ref[...] + jnp.dot(merged.astype(BF16), wo_ref[...], preferred_element_type=F32)
    ro_ref[...] = r_new
    rn_sc[...] = r_new

    rank0 = jnp.sum(jnp.where(lane == i0, rank, 0.0), axis=-1, keepdims=True)
    rank1 = jnp.sum(jnp.where(lane == i1, rank, 0.0), axis=-1, keepdims=True)
    cnt_sc[0:1, :] = cnt_sc[0:1, :] + jnp.sum(onehot, axis=0, keepdims=True)
    cnt_ref[...] = jnp.broadcast_to(cnt_sc[0:1, :], cnt_ref.shape)

    w0 = jnp.where(valid, w0, 0.0)
    w1 = jnp.where(valid, w1, 0.0)
    out = jnp.where(lane == 0, (i0 - first_e).astype(F32), 0.0)
    out = jnp.where(lane == 1, (i1 - first_e).astype(F32), out)
    out = jnp.where(lane == 2, w0, out)
    out = jnp.where(lane == 3, w1, out)
    out = jnp.where(lane == 4, rank0, out)
    out = jnp.where(lane == 5, rank1, out)
    route_ref[...] = out
    route_t_ref[...] = out.T[:ROUTE_FIELDS, :]


def _merge_route(att, cv, z, col_ga, col_gc, r, wao, wco, wo, gain, wr, br, tiles_per_seq):
    n_tok, d = r.shape
    aw = att.shape[1]
    cw = cv.shape[1]
    tm = SEQ_TILE
    half = d // 2
    gab, gcb = col_ga // half, col_gc // half
    const = lambda i: (0, 0)
    single = pl.Buffered(1)
    vmem = (2 * tm * (aw * 2 + cw * 2 + 2 * d * 2 + 3 * d * 4 + LANES * 4)
            + (aw * d + cw * d + d * d + d * LANES) * 2 + 8 * tm * d * 4)
    n_tiles = n_tok // tm
    cur = lambda i: jnp.minimum(i, n_tiles - 1)
    prev = lambda i: jnp.maximum(i - 1, 0)
    return pl.pallas_call(
        functools.partial(_merge_route_kernel, tiles_per_seq=tiles_per_seq),
        grid=(n_tiles + 1,),
        in_specs=[
            pl.BlockSpec((tm, aw), lambda i: (cur(i), 0)),
            pl.BlockSpec((tm, cw), lambda i: (cur(i), 0)),
            pl.BlockSpec((tm, half), lambda i: (cur(i), gab)),
            pl.BlockSpec((tm, half), lambda i: (cur(i), gab + 1)),
            pl.BlockSpec((tm, half), lambda i: (cur(i), gcb)),
            pl.BlockSpec((tm, half), lambda i: (cur(i), gcb + 1)),
            pl.BlockSpec((tm, d), lambda i: (cur(i), 0)),
            pl.BlockSpec(wao.shape, const, pipeline_mode=single),
            pl.BlockSpec(wco.shape, const, pipeline_mode=single),
            pl.BlockSpec(wo.shape, const, pipeline_mode=single),
            pl.BlockSpec((1, d), const),
            pl.BlockSpec(wr.shape, const, pipeline_mode=single),
            pl.BlockSpec((1, LANES), const),
        ],
        out_specs=[
            pl.BlockSpec((tm, d), lambda i: (cur(i), 0)),
            pl.BlockSpec((tm, d // 2), lambda i: (prev(i), 0)),
            pl.BlockSpec((tm, LANES), lambda i: (prev(i), 0)),
            pl.BlockSpec((ROUTE_FIELDS, tm), lambda i: (0, prev(i))),
            pl.BlockSpec((8, LANES), const),
        ],
        out_shape=[
            jax.ShapeDtypeStruct((n_tok, d), F32),
            jax.ShapeDtypeStruct((n_tok, d // 2), U32),
            jax.ShapeDtypeStruct((n_tok, LANES), F32),
            jax.ShapeDtypeStruct((ROUTE_FIELDS, n_tok), F32),
            jax.ShapeDtypeStruct((8, LANES), F32),
        ],
        scratch_shapes=[pltpu.VMEM((8, LANES), F32), pltpu.VMEM((tm, d), F32)],
        compiler_params=pltpu.CompilerParams(
            dimension_semantics=("arbitrary",), vmem_limit_bytes=_vmem_limit(vmem + tm * d * 4)),
        name="merge_route",
    )(att, cv, z, z, z, z, r, wao, wco, wo, gain, wr, br)


def _dispatch_kernel(dest_ref, zflag_ref, n_ref, xs_ref, zero_sc, stage, sem, zsem, *, tiles_per_seq):
    i = pl.program_id(0)
    tm = zero_sc.shape[0]
    n_tiles = xs_ref.shape[0] // tm
    n_tok = pl.num_programs(0) * tm

    def zero_copy(t):
        return pltpu.make_async_copy(zero_sc, xs_ref.at[pl.ds(pl.multiple_of(t * tm, tm), tm)], zsem)

    @pl.when(i == 0)
    def _():
        zero_sc[...] = jnp.zeros_like(zero_sc)

        def issue_zero(t, _):
            @pl.when(zflag_ref[t] != 0)
            def _():
                zero_copy(t).start()
            return 0

        def drain_zero(t, _):
            @pl.when(zflag_ref[t] != 0)
            def _():
                zero_copy(t).wait()
            return 0

        lax.fori_loop(0, n_tiles, issue_zero, 0)
        lax.fori_loop(0, n_tiles, drain_zero, 0)

    slot = i % 2
    last = pl.num_programs(0) - 1

    def first_row_group(step):
        return jnp.where(step % tiles_per_seq == 0, META0, 0) // SUBLANES

    def row_copy(step, buf, g, u, k):
        d = dest_ref[k * n_tok + step * tm + g * SUBLANES + u]
        return pltpu.make_async_copy(stage.at[buf, g, pl.ds(u, 1)], xs_ref.at[pl.ds(d, 1)], sem.at[buf])

    def drain(step, buf):
        one_row = pltpu.make_async_copy(stage.at[buf, 0, pl.ds(0, 1)], xs_ref.at[pl.ds(0, 1)], sem.at[buf])

        def body(g, _):
            for _u in range(2 * SUBLANES):
                one_row.wait()
            return 0
        lax.fori_loop(first_row_group(step), tm // SUBLANES, body, 0)

    @pl.when(i >= 2)
    def _():
        drain(i - 2, slot)

    stage[slot] = n_ref[...]

    def issue(g, _):
        for u in range(SUBLANES):
            row_copy(i, slot, g, u, 0).start()
            row_copy(i, slot, g, u, 1).start()
        return 0

    lax.fori_loop(first_row_group(i), tm // SUBLANES, issue, 0)

    @pl.when(i == last)
    def _():
        @pl.when(i >= 1)
        def _():
            drain(i - 1, 1 - slot)
        drain(i, slot)


def _dispatch(dest, zflag, n, n_rows, tiles_per_seq):
    n_tok, w = n.shape
    tm = MOE_TM
    groups = tm // SUBLANES
    return pl.pallas_call(
        functools.partial(_dispatch_kernel, tiles_per_seq=tiles_per_seq),
        grid_spec=pltpu.PrefetchScalarGridSpec(
            num_scalar_prefetch=2,
            grid=(n_tok // tm,),
            in_specs=[pl.BlockSpec((groups, SUBLANES, w), lambda i, *_: (i, 0, 0))],
            out_specs=pl.BlockSpec(memory_space=pl.ANY),
            scratch_shapes=[pltpu.VMEM((tm, w), n.dtype), pltpu.VMEM((2, groups, SUBLANES, w), n.dtype),
                            pltpu.SemaphoreType.DMA((2,)), pltpu.SemaphoreType.DMA],
        ),
        out_shape=jax.ShapeDtypeStruct((n_rows, w), n.dtype),
        compiler_params=pltpu.CompilerParams(dimension_semantics=("arbitrary",),
                                             has_side_effects=True),
        name="moe_dispatch",
    )(dest, zflag, n.reshape(n_tok // SUBLANES, SUBLANES, w))


def _expert_kernel(te_ref, first_ref, nxt_ref, na_ref, x_ref, wgu_hbm, wdn_hbm, y_ref,
                   gu_stage, dn_stage, gu_bf, dn_bf, sem, *, layer):
    i = pl.program_id(0)
    active = i < na_ref[0]

    def weight_copies(e):
        return (pltpu.make_async_copy(wgu_hbm.at[layer, e], gu_stage, sem.at[0]),
                pltpu.make_async_copy(wdn_hbm.at[layer, e], dn_stage, sem.at[1]))

    @pl.when(i == 0)
    def _():
        for cp in weight_copies(te_ref[0]):
            cp.start()

    @pl.when(active & (first_ref[i] != 0))
    def _():
        for cp in weight_copies(te_ref[i]):
            cp.wait()
        gu_bf[...] = gu_stage[...].astype(BF16)
        dn_bf[...] = dn_stage[...].astype(BF16)

        @pl.when(nxt_ref[i] >= 0)
        def _():
            for cp in weight_copies(nxt_ref[i]):
                cp.start()

    @pl.when(active)
    def _():
        de = dn_bf.shape[0]
        x = _unpack_rows(x_ref[...]).astype(BF16)
        h = jnp.dot(x, gu_bf[...], preferred_element_type=F32)
        a = h[:, :de]
        b = h[:, de:]
        act = a * _sigmoid(a) * b
        y_ref[...] = _pack_rows(jnp.dot(act.astype(BF16), dn_bf[...], preferred_element_type=F32))

    @pl.when(jnp.logical_not(active))
    def _():
        y_ref[...] = jnp.zeros_like(y_ref)


def _experts(tile_e, first, nxt, n_act, xs, w_gate_up, w_down, layer):
    n_rows, w = xs.shape
    d = w_gate_up.shape[2]
    assert d == 2 * w
    tm = MOE_TM
    de2 = w_gate_up.shape[3]
    de = w_down.shape[2]
    row_map = lambda i, te, fi, nx, na: (jnp.minimum(i, na[0] - 1), 0)
    vmem = (d * de2 + de * d) * (4 + 2) + 6 * tm * d * 4 + 6 * tm * de2 * 4
    return pl.pallas_call(
        functools.partial(_expert_kernel, layer=layer),
        grid_spec=pltpu.PrefetchScalarGridSpec(
            num_scalar_prefetch=4,
            grid=(n_rows // tm,),
            in_specs=[
                pl.BlockSpec((tm, w), row_map),
                pl.BlockSpec(memory_space=pl.ANY),
                pl.BlockSpec(memory_space=pl.ANY),
            ],
            out_specs=pl.BlockSpec((tm, w), lambda i, *_: (i, 0)),
            scratch_shapes=[
                pltpu.VMEM((d, de2), F32), pltpu.VMEM((de, d), F32),
                pltpu.VMEM((d, de2), BF16), pltpu.VMEM((de, d), BF16),
                pltpu.SemaphoreType.DMA((2,)),
            ],
        ),
        out_shape=jax.ShapeDtypeStruct((n_rows, w), xs.dtype),
        compiler_params=pltpu.CompilerParams(
            dimension_semantics=("arbitrary",), vmem_limit_bytes=_vmem_limit(vmem)),
        name="moe_experts",
    )(tile_e, first, nxt, n_act, xs, w_gate_up, w_down)


def _combine_kernel(dest_ref, r_ref, route_ref, y_ref, *rest, tile_of, final, n_tok):
    if final:
        g_ref, o_ref, ybuf, sem = rest
    else:
        o_ref, ybuf, sem = rest
    tm = r_ref.shape[0]
    i = pl.program_id(0)
    slot = i % 2

    def issue(step, buf):
        tile = tile_of(step)

        def body(g, _):
            for u in range(SUBLANES):
                for k in range(2):
                    d = dest_ref[k * n_tok + tile * tm + g * SUBLANES + u]
                    pltpu.make_async_copy(y_ref.at[pl.ds(d, 1)], ybuf.at[buf, k, g, pl.ds(u, 1)],
                                          sem.at[buf]).start()
            return 0
        lax.fori_loop(0, tm // SUBLANES, body, 0)

    @pl.when(i == 0)
    def _():
        issue(i, slot)

    @pl.when(i + 1 < pl.num_programs(0))
    def _():
        issue(i + 1, 1 - slot)

    one_row = pltpu.make_async_copy(y_ref.at[pl.ds(0, 1)], ybuf.at[slot, 0, 0, pl.ds(0, 1)], sem.at[slot])

    def drain(g, _):
        for _u in range(2 * SUBLANES):
            one_row.wait()
        return 0
    lax.fori_loop(0, tm // SUBLANES, drain, 0)

    route = route_ref[...]
    w = ybuf.shape[-1]
    out = (r_ref[...] + route[:, 2:3] * _unpack_rows(ybuf[slot, 0].reshape(tm, w))
           + route[:, 3:4] * _unpack_rows(ybuf[slot, 1].reshape(tm, w)))
    if final:
        ms = jnp.mean(out * out, axis=-1, keepdims=True)
        out = out * lax.rsqrt(ms + RMS_EPS) * g_ref[...]
    o_ref[...] = out


def _combine(dest, r, route, y, tiles_per_seq, final_gain=None):
    n_tok, d = r.shape
    tm = MOE_TM
    final = final_gain is not None
    if final:
        real_tiles = tiles_per_seq - P0 // tm
        tile_of = lambda i: (i // real_tiles) * tiles_per_seq + P0 // tm + i % real_tiles
        n_out_tiles = (n_tok // tm // tiles_per_seq) * real_tiles
    else:
        tile_of = lambda i: i
        n_out_tiles = n_tok // tm
    in_specs = [
        pl.BlockSpec((tm, d), lambda i, *_: (tile_of(i), 0)),
        pl.BlockSpec((tm, LANES), lambda i, *_: (tile_of(i), 0)),
        pl.BlockSpec(memory_space=pl.ANY),
    ]
    args = [dest, r, route, y]
    if final:
        in_specs.append(pl.BlockSpec((1, d), lambda i, *_: (0, 0)))
        args.append(final_gain)
    return pl.pallas_call(
        functools.partial(_combine_kernel, tile_of=tile_of, final=final, n_tok=n_tok),
        grid_spec=pltpu.PrefetchScalarGridSpec(
            num_scalar_prefetch=1,
            grid=(n_out_tiles,),
            in_specs=in_specs,
            out_specs=pl.BlockSpec((tm, d), lambda i, *_: (i, 0)),
            scratch_shapes=[pltpu.VMEM((2, 2, tm // SUBLANES, SUBLANES, d // 2), y.dtype),
                            pltpu.SemaphoreType.DMA((2,))],
        ),
        out_shape=jax.ShapeDtypeStruct((n_out_tiles * tm, d), F32),
        compiler_params=pltpu.CompilerParams(
            dimension_semantics=("arbitrary",),
            vmem_limit_bytes=_vmem_limit(8 * tm * d * 4 + 4 * tm * d * 4 + (8 << 20))),
        name="moe_combine_final" if final else "moe_combine",
    )(*args)


def _routing_tables(route_t, counts, n_tiles_max, valid_tok):
    tm = MOE_TM
    cnt = counts[0, ROUTE_FIRST_EXPERT_LANE:ROUTE_FIRST_EXPERT_LANE + N_EXPERTS].astype(I32)
    ntile = (cnt + tm - 1) // tm
    tile_end = jnp.cumsum(ntile)
    tile_start = tile_end - ntile
    n_act = tile_end[-1:]
    e_ids = route_t[0:2].astype(I32)
    rank = route_t[4:6].astype(I32)
    dest = jnp.where(valid_tok[None, :], tile_start[e_ids] * tm + rank, 0).reshape(-1)
    all_tiles = jnp.arange(n_tiles_max, dtype=I32)
    clamped = jnp.minimum(all_tiles, n_act[0] - 1)
    tile_e = jnp.sum((clamped[:, None] >= tile_end[None, :]).astype(I32), axis=1)
    tile_e = jnp.minimum(tile_e, N_EXPERTS - 1)
    first = jnp.concatenate([jnp.ones((1,), I32), (tile_e[1:] != tile_e[:-1]).astype(I32)])
    end_of_mine = tile_end[tile_e]
    nxt = jnp.where(end_of_mine < n_act[0], tile_e[jnp.minimum(end_of_mine, n_tiles_max - 1)], -1)
    is_expert_tail = jnp.any((all_tiles[:, None] == tile_end[None, :] - 1) & (ntile[None, :] > 0), axis=1)
    zflag = (is_expert_tail | (all_tiles >= n_act[0])).astype(I32)
    return dest, tile_e, first, nxt.astype(I32), n_act.astype(I32), zflag


def kernel(x, meta, norm_mix, w_in, b_forget, w_attn_out, conv_w, conv_b, conv_ln_g, conv_ln_b,
           w_conv_out, w_out, norm_ffn, w_router_group, b_router_group, w_router_expert,
           b_router_expert, w_gate_up, w_down, norm_final):
    batch, seq, d = x.shape
    depth = w_in.shape[0]
    heads = b_forget.shape[1]
    att_w = heads * HEAD_DIM
    conv_c = conv_b.shape[1]
    lp = P0 + seq
    tiles_per_seq = lp // SEQ_TILE
    n_tok = batch * lp
    assert seq % SEQ_TILE == 0 and n_tok % INPROJ_TM == 0
    assert (3 * att_w) % INPROJ_TN == 0 and (2 * conv_c + 2 * d) % INPROJ_TN == 0 and d % 2 == 0
    assert att_w % LANES == 0 and conv_c % LANES == 0 and SEQ_TILE == MOE_TM

    pos = jnp.arange(lp)
    is_meta = (pos >= META0) & (pos < P0)
    meta_rows = jnp.pad(meta.astype(x.dtype), ((META0, lp - P0), (0, 0)))
    r = jnp.where(is_meta[None, :, None], meta_rows[None], jnp.pad(x, ((0, 0), (P0, 0), (0, 0))))
    r = r.reshape(n_tok, d)

    c_f = 3 * att_w
    c_u = c_f + heads
    m_q, m_k, m_v = 0, att_w, 2 * att_w
    m_ua = 3 * att_w
    m_ub = m_ua + conv_c
    m_ga = m_ub + conv_c
    m_gc = m_ga + d

    valid_tok = jnp.tile(jnp.arange(lp) >= META0, batch)
    n_pairs = 2 * batch * (lp - META0)
    n_tiles_max = n_pairs // MOE_TM + N_EXPERTS
    n_rows = n_tiles_max * MOE_TM

    out = None
    for l in range(depth):
        wf = jnp.pad(w_in[l, :, c_f:c_u], ((0, 0), (0, LANES - heads))).astype(BF16)
        z, f = _inproj(r, norm_mix[l][None], w_in[l, :, :c_f].astype(BF16),
                       w_in[l, :, c_u:].astype(BF16), wf)
        cum = _forget_cumsum(f.reshape(batch, lp, LANES),
                             jnp.pad(b_forget[l], (0, LANES - heads))[None])
        z3 = z.reshape(batch, lp, -1)
        att = _attention(z3, cum, m_q, m_k, m_v).reshape(n_tok, att_w)
        conv_w_pad = jnp.concatenate([conv_w[l], jnp.zeros((1, conv_c), F32)], axis=0)
        cv = _conv_branch(z3, m_ua, m_ub, conv_w_pad, conv_b[l][None], conv_ln_g[l][None],
                          conv_ln_b[l][None]).reshape(n_tok, conv_c)

        wr = jnp.zeros((d, LANES), F32)
        wr = wr.at[:, :N_GROUPS].set(w_router_group[l])
        wr = wr.at[:, N_GROUPS:N_GROUPS + N_EXPERTS].set(w_router_expert[l]).astype(BF16)
        br = jnp.zeros((1, LANES), F32)
        br = br.at[0, :N_GROUPS].set(b_router_group[l])
        br = br.at[0, N_GROUPS:N_GROUPS + N_EXPERTS].set(b_router_expert[l])
        r, n2, route, route_t, counts = _merge_route(
            att, cv, z, m_ga, m_gc, r, w_attn_out[l].astype(BF16), w_conv_out[l].astype(BF16),
            w_out[l].astype(BF16), norm_ffn[l][None], wr, br, tiles_per_seq)

        dest, tile_e, first, nxt, n_act, zflag = _routing_tables(route_t, counts, n_tiles_max, valid_tok)
        xs = _dispatch(dest, zflag, n2, n_rows, tiles_per_seq)
        y = _experts(tile_e, first, nxt, n_act, xs, w_gate_up, w_down, l)
        if l + 1 < depth:
            r = _combine(dest, r, route, y, tiles_per_seq)
        else:
            out = _combine(dest, r, route, y, tiles_per_seq, final_gain=norm_final[None])
    return out.reshape(batch, seq, d)
```

```python
import functools

import jax
import jax.numpy as jnp
from jax import lax
from jax.experimental import pallas as pl
from jax.experimental.pallas import tpu as pltpu

F32 = jnp.float32
BF16 = jnp.bfloat16
I32 = jnp.int32
U32 = jnp.uint32

LANES = 128
SUBLANES = 8
MXU_DIM = 256
VMEM_BYTES_V7X = 64 * 1024 * 1024

N_META = 16
HEAD_DIM = 64
HEADS_PER_BLOCK = LANES // HEAD_DIM
ATT_HEADS = 16
ATT_BLOCK = ATT_HEADS * HEAD_DIM
CONV_K = 31
N_GROUPS = 4
EXPERTS_PER_GROUP = 8
N_EXPERTS = N_GROUPS * EXPERTS_PER_GROUP
LOG2_E = 1.4426950408889634
RMS_EPS = 1e-6
LN_EPS = 1e-5

SEQ_TILE = 256
P0 = SEQ_TILE
META0 = P0 - N_META
INPROJ_TM = 1024
INPROJ_TN = 1024
MOE_TM = 256
CONV_HALO = 32
DMA_UNROLL = 8
ROUTE_FIRST_EXPERT_LANE = N_GROUPS
ROUTE_FIELDS = 8
MASK_VALUE = -1e30


def _vmem_limit(nbytes):
    return int(min(max(nbytes, 16 * 1024 * 1024), VMEM_BYTES_V7X - 6 * 1024 * 1024))


def _sigmoid(x):
    return 1.0 / (1.0 + jnp.exp(-x))


def _pack_rows(v):
    c = v.shape[1] // 2
    lo = lax.bitcast_convert_type(v[:, :c].astype(BF16).astype(F32), U32)
    hi = lax.bitcast_convert_type(v[:, c:].astype(BF16).astype(F32), U32)
    return (hi & jnp.uint32(0xFFFF0000)) | (lo >> 16)


def _unpack_rows(w):
    lo = lax.bitcast_convert_type(w << 16, F32)
    hi = lax.bitcast_convert_type(w & jnp.uint32(0xFFFF0000), F32)
    return jnp.concatenate([lo, hi], axis=1)


def _inproj_kernel(r_ref, g_ref, wa_ref, wb_ref, wf_ref, z_ref, f_ref, n_sc, *, na):
    j = pl.program_id(1)

    @pl.when(j == 0)
    def _():
        x = r_ref[...]
        ms = jnp.mean(x * x, axis=-1, keepdims=True)
        n = (x * lax.rsqrt(ms + RMS_EPS) * g_ref[...]).astype(BF16)
        n_sc[...] = n
        f_ref[...] = jnp.dot(n, wf_ref[...], preferred_element_type=F32)

    @pl.when(j < na)
    def _():
        z_ref[...] = jnp.dot(n_sc[...], wa_ref[...], preferred_element_type=F32).astype(z_ref.dtype)

    @pl.when(j >= na)
    def _():
        z_ref[...] = jnp.dot(n_sc[...], wb_ref[...], preferred_element_type=F32).astype(z_ref.dtype)


def _inproj(r, gain, w_a, w_b, wf):
    n_tok, d = r.shape
    tm, tn = INPROJ_TM, INPROJ_TN
    na, nb = w_a.shape[1] // tn, w_b.shape[1] // tn
    cols = (na + nb) * tn
    vmem = 2 * tm * d * 4 + 4 * d * tn * 2 + 2 * tm * tn * 2 + tm * d * 2 + 2 * (d * 2 + tm * 4) * LANES
    return pl.pallas_call(
        functools.partial(_inproj_kernel, na=na),
        grid=(n_tok // tm, na + nb),
        in_specs=[
            pl.BlockSpec((tm, d), lambda i, j: (i, 0)),
            pl.BlockSpec((1, d), lambda i, j: (0, 0)),
            pl.BlockSpec((d, tn), lambda i, j: (0, jnp.minimum(j, na - 1))),
            pl.BlockSpec((d, tn), lambda i, j: (0, jnp.maximum(j - na, 0))),
            pl.BlockSpec((d, LANES), lambda i, j: (0, 0)),
        ],
        out_specs=[
            pl.BlockSpec((tm, tn), lambda i, j: (i, j)),
            pl.BlockSpec((tm, LANES), lambda i, j: (i, 0)),
        ],
        out_shape=[
            jax.ShapeDtypeStruct((n_tok, cols), BF16),
            jax.ShapeDtypeStruct((n_tok, LANES), F32),
        ],
        scratch_shapes=[pltpu.VMEM((tm, d), BF16)],
        compiler_params=pltpu.CompilerParams(
            dimension_semantics=("parallel", "arbitrary"),
            vmem_limit_bytes=_vmem_limit(vmem + (8 << 20))),
        name="inproj",
    )(r, gain, w_a, w_b, wf)


def _cum_kernel(f_ref, b_ref, o_ref):
    ch = MXU_DIM
    lp = f_ref.shape[1]
    row = lax.broadcasted_iota(I32, (ch, ch), 0)
    col = lax.broadcasted_iota(I32, (ch, ch), 1)
    lower = (col <= row).astype(F32)
    carry = jnp.zeros((1, LANES), F32)
    for c in range(lp // ch):
        x = f_ref[0, c * ch:(c + 1) * ch, :] + b_ref[...]
        log_f = jnp.minimum(x, 0.0) - jnp.log1p(jnp.exp(-jnp.abs(x)))
        loc = jnp.dot(lower, log_f, preferred_element_type=F32,
                      precision=lax.Precision.HIGHEST) + carry
        o_ref[0, c * ch:(c + 1) * ch, :] = loc
        carry = loc[ch - 1:ch, :]


def _forget_cumsum(f3, b_forget):
    batch, lp, _ = f3.shape
    return pl.pallas_call(
        _cum_kernel,
        grid=(batch,),
        in_specs=[
            pl.BlockSpec((1, lp, LANES), lambda b: (b, 0, 0)),
            pl.BlockSpec((1, LANES), lambda b: (0, 0)),
        ],
        out_specs=pl.BlockSpec((1, lp, LANES), lambda b: (b, 0, 0)),
        out_shape=jax.ShapeDtypeStruct((batch, lp, LANES), F32),
        compiler_params=pltpu.CompilerParams(dimension_semantics=("parallel",)),
        name="forget_cumsum",
    )(f3, b_forget)


def _attn_kernel(q_ref, k_ref, v_ref, c_ref, o_ref, kaug_sc, vt_sc, qaug_sc, s_a, s_b,
                 m_sc, l_sc, acc_sc):
    tq = q_ref.shape[1]
    tk = tq
    lp = k_ref.shape[1]
    hg = pl.program_id(1)
    qi = pl.program_id(2)
    lane = lax.broadcasted_iota(I32, (1, LANES), 1)
    nt_dims = (((1,), (1,)), ((), ()))

    def own_lanes(h):
        return lane < HEAD_DIM if h % HEADS_PER_BLOCK == 0 else lane >= HEAD_DIM

    def bias_lane(h):
        return HEAD_DIM if h % HEADS_PER_BLOCK == 0 else 0

    def block_lanes(h):
        blk = h // HEADS_PER_BLOCK
        return slice(blk * LANES, (blk + 1) * LANES)

    @pl.when(qi == 0)
    def _():
        for c in range(lp // tk):
            rows = slice(c * tk, (c + 1) * tk)
            vt_sc[:, rows] = v_ref[0, rows, :].astype(F32).T.astype(BF16)
            cum = c_ref[0, rows, :]
            key_pos = c * tk + lax.broadcasted_iota(I32, (tk, 1), 0)
            for h in range(ATT_HEADS):
                head = hg * ATT_HEADS + h
                col = jnp.sum(jnp.where(lane == head, cum, 0.0), axis=-1, keepdims=True) * LOG2_E
                hi = col.astype(BF16).astype(F32)
                rem = col - hi
                mid = rem.astype(BF16).astype(F32)
                low = rem - mid
                hi = jnp.where(key_pos < META0, -MASK_VALUE, hi)
                a = bias_lane(h)
                bias = jnp.where(lane == a, hi, jnp.where(lane == a + 1, mid,
                                                          jnp.where(lane == a + 2, low, 0.0)))
                kaug_sc[h, rows, :] = jnp.where(own_lanes(h), k_ref[0, rows, block_lanes(h)],
                                                bias.astype(BF16))

    for h in range(ATT_HEADS):
        a = bias_lane(h)
        minus_one = jnp.where((lane >= a) & (lane < a + 3), -1.0, 0.0)
        scaled = q_ref[0, :, block_lanes(h)].astype(F32) * (HEAD_DIM ** -0.5 * LOG2_E)
        qaug_sc[h] = jnp.where(own_lanes(h), scaled, minus_one).astype(BF16)
        m_sc[h] = jnp.full((1, tq), MASK_VALUE, F32)
        l_sc[h] = jnp.zeros((1, tq), F32)
        acc_sc[h] = jnp.zeros((HEAD_DIM, tq), F32)

    def scores_into(dst, j):
        s0 = pl.multiple_of(j * tk, tk)
        for h in range(ATT_HEADS):
            dst[h] = lax.dot_general(kaug_sc[h, pl.ds(s0, tk), :], qaug_sc[h], nt_dims,
                                     preferred_element_type=F32)

    def softmax_pv(src, j, diagonal):
        s0 = pl.multiple_of(j * tk, tk)
        for h in range(ATT_HEADS):
            st = src[h]
            if diagonal:
                key = lax.broadcasted_iota(I32, (tk, tq), 0)
                qry = lax.broadcasted_iota(I32, (tk, tq), 1)
                st = jnp.where(key <= qry, st, MASK_VALUE)
            m = m_sc[h]
            m_new = jnp.maximum(m, jnp.max(st, axis=0, keepdims=True))
            alpha = jnp.exp2(m - m_new)
            p = jnp.exp2(st - m_new)
            l_sc[h] = alpha * l_sc[h] + jnp.sum(p, axis=0, keepdims=True)
            pv = jnp.dot(vt_sc[h * HEAD_DIM:(h + 1) * HEAD_DIM, pl.ds(s0, tk)], p.astype(BF16),
                         preferred_element_type=F32)
            acc_sc[h] = alpha * acc_sc[h] + pv
            m_sc[h] = m_new

    scores_into(s_a, 0)

    def pair(i, _):
        j = 2 * i
        scores_into(s_b, j + 1)
        softmax_pv(s_a, j, False)
        scores_into(s_a, j + 2)
        softmax_pv(s_b, j + 1, False)
        return 0

    lax.fori_loop(0, qi // 2, pair, 0)

    @pl.when(qi % 2 == 0)
    def _():
        softmax_pv(s_a, qi, True)

    @pl.when(qi % 2 == 1)
    def _():
        scores_into(s_b, qi)
        softmax_pv(s_a, qi - 1, False)
        softmax_pv(s_b, qi, True)

    att_t = jnp.concatenate([acc_sc[h] / l_sc[h] for h in range(ATT_HEADS)], axis=0)
    o_ref[0] = att_t.T.astype(o_ref.dtype)


def _attention(z3, cum, col_q, col_k, col_v):
    batch, lp, _ = z3.shape
    att_w = col_k - col_q
    tq = SEQ_TILE
    qb, kb, vb = col_q // ATT_BLOCK, col_k // ATT_BLOCK, col_v // ATT_BLOCK
    vmem = (2 * (2 * lp + 2 * tq) * ATT_BLOCK * 2 + 2 * lp * LANES * 4
            + (ATT_HEADS * lp * LANES + ATT_BLOCK * lp + ATT_HEADS * tq * LANES) * 2
            + (2 * ATT_HEADS * tq * tq + ATT_HEADS * (HEAD_DIM + 2 * SUBLANES) * tq) * 4
            + (8 << 20))
    return pl.pallas_call(
        _attn_kernel,
        grid=(batch, att_w // ATT_BLOCK, lp // tq),
        in_specs=[
            pl.BlockSpec((1, tq, ATT_BLOCK), lambda b, h, i: (b, i, qb + h)),
            pl.BlockSpec((1, lp, ATT_BLOCK), lambda b, h, i: (b, 0, kb + h)),
            pl.BlockSpec((1, lp, ATT_BLOCK), lambda b, h, i: (b, 0, vb + h)),
            pl.BlockSpec((1, lp, LANES), lambda b, h, i: (b, 0, 0)),
        ],
        out_specs=pl.BlockSpec((1, tq, ATT_BLOCK), lambda b, h, i: (b, i, h)),
        out_shape=jax.ShapeDtypeStruct((batch, lp, att_w), BF16),
        scratch_shapes=[
            pltpu.VMEM((ATT_HEADS, lp, LANES), BF16), pltpu.VMEM((ATT_BLOCK, lp), BF16),
            pltpu.VMEM((ATT_HEADS, tq, LANES), BF16),
            pltpu.VMEM((ATT_HEADS, tq, tq), F32), pltpu.VMEM((ATT_HEADS, tq, tq), F32),
            pltpu.VMEM((ATT_HEADS, 1, tq), F32), pltpu.VMEM((ATT_HEADS, 1, tq), F32),
            pltpu.VMEM((ATT_HEADS, HEAD_DIM, tq), F32),
        ],
        compiler_params=pltpu.CompilerParams(
            dimension_semantics=("parallel", "parallel", "arbitrary"),
            vmem_limit_bytes=_vmem_limit(vmem)),
        name="fox_attention",
    )(z3, z3, z3, cum)


def _conv_kernel(a_ref, b_ref, w_ref, cb_ref, g_ref, lb_ref, o_ref, zbuf, ybuf, zshift):
    tt = a_ref.shape[1]
    ch = a_ref.shape[2]
    t = pl.program_id(1)

    @pl.when(t == 0)
    def _():
        zbuf[0:CONV_HALO, :] = jnp.zeros((CONV_HALO, ch), F32)

    z = a_ref[0].astype(F32) * _sigmoid(b_ref[0].astype(F32))
    rows = t * tt + lax.broadcasted_iota(I32, (tt, 1), 0)
    zbuf[CONV_HALO:CONV_HALO + tt, :] = jnp.where(rows >= META0, z, 0.0)

    first = CONV_HALO - (CONV_K - 1)

    def chan_block(cb, _):
        c0 = pl.multiple_of(cb * LANES, LANES)
        for res in range(SUBLANES):
            span = tt + ((CONV_K - 1 - res) // SUBLANES) * SUBLANES
            zshift[res, 0:span, :] = zbuf[first + res:first + res + span, pl.ds(c0, LANES)]
        acc = jnp.zeros((tt, LANES), F32) + cb_ref[:, pl.ds(c0, LANES)]
        for k in range(CONV_K):
            base = (k // SUBLANES) * SUBLANES
            acc = acc + w_ref[k:k + 1, pl.ds(c0, LANES)] * zshift[k % SUBLANES, base:base + tt, :]
        ybuf[:, pl.ds(c0, LANES)] = acc
        return 0

    lax.fori_loop(0, ch // LANES, chan_block, 0)

    y = ybuf[...]
    mu = jnp.mean(y, axis=-1, keepdims=True)
    yc = y - mu
    var = jnp.mean(yc * yc, axis=-1, keepdims=True)
    zn = yc * lax.rsqrt(var + LN_EPS) * g_ref[...] + lb_ref[...]
    o_ref[0] = (zn * _sigmoid(zn)).astype(o_ref.dtype)
    zbuf[0:CONV_HALO, :] = zbuf[tt:tt + CONV_HALO, :]


def _conv_branch(z3, col_a, col_b, conv_w, conv_b, ln_g, ln_b):
    batch, lp, _ = z3.shape
    ch = conv_b.shape[1]
    tt = SEQ_TILE
    ab, bb = col_a // ch, col_b // ch
    return pl.pallas_call(
        _conv_kernel,
        grid=(batch, lp // tt),
        in_specs=[
            pl.BlockSpec((1, tt, ch), lambda b, t: (b, t, ab)),
            pl.BlockSpec((1, tt, ch), lambda b, t: (b, t, bb)),
            pl.BlockSpec(conv_w.shape, lambda b, t: (0, 0)),
            pl.BlockSpec((1, ch), lambda b, t: (0, 0)),
            pl.BlockSpec((1, ch), lambda b, t: (0, 0)),
            pl.BlockSpec((1, ch), lambda b, t: (0, 0)),
        ],
        out_specs=pl.BlockSpec((1, tt, ch), lambda b, t: (b, t, 0)),
        out_shape=jax.ShapeDtypeStruct((batch, lp, ch), BF16),
        scratch_shapes=[
            pltpu.VMEM((tt + CONV_HALO, ch), F32), pltpu.VMEM((tt, ch), F32),
            pltpu.VMEM((SUBLANES, tt + ((CONV_K - 1) // SUBLANES) * SUBLANES, LANES), F32),
        ],
        compiler_params=pltpu.CompilerParams(dimension_semantics=("parallel", "arbitrary")),
        name="conv_branch",
    )(z3, z3, conv_w, conv_b, ln_g, ln_b)


def _merge_route_kernel(att_ref, cv_ref, ga0_ref, ga1_ref, gc0_ref, gc1_ref, r_ref, wao_ref, wco_ref,
                        wo_ref, g_ref, wr_ref, br_ref, ro_ref, n_ref, route_ref, route_t_ref, cnt_ref,
                        cnt_sc, rn_sc, *, tiles_per_seq):
    i = pl.program_id(0)
    tm = r_ref.shape[0]

    @pl.when(i == 0)
    def _():
        cnt_sc[...] = jnp.zeros_like(cnt_sc)
        rn_sc[...] = jnp.zeros_like(rn_sc)

    br_att = jnp.dot(att_ref[...], wao_ref[...], preferred_element_type=F32)

    r_prev = rn_sc[...]
    ms = jnp.mean(r_prev * r_prev, axis=-1, keepdims=True)
    n = r_prev * lax.rsqrt(ms + RMS_EPS) * g_ref[...]
    n_ref[...] = _pack_rows(n)
    logits = jnp.dot(n.astype(BF16), wr_ref[...], preferred_element_type=F32) + br_ref[...]

    br_conv = jnp.dot(cv_ref[...], wco_ref[...], preferred_element_type=F32)
    lane = lax.broadcasted_iota(I32, logits.shape, 1)
    big = jnp.int32(4 * LANES)
    first_e = ROUTE_FIRST_EXPERT_LANE

    gl = jnp.where(lane < N_GROUPS, logits, -jnp.inf)
    gmax = jnp.max(gl, axis=-1, keepdims=True)
    gsum = jnp.sum(jnp.exp(gl - gmax), axis=-1, keepdims=True)
    g_w = 1.0 / gsum
    g_idx = jnp.min(jnp.where(gl == gmax, lane, big), axis=-1, keepdims=True)

    in_group = (lane >= first_e + g_idx * EXPERTS_PER_GROUP) & \
               (lane < first_e + (g_idx + 1) * EXPERTS_PER_GROUP)
    el = jnp.where(in_group, logits, -jnp.inf)
    v0 = jnp.max(el, axis=-1, keepdims=True)
    i0 = jnp.min(jnp.where(el == v0, lane, big), axis=-1, keepdims=True)
    el = jnp.where(lane == i0, -jnp.inf, el)
    v1 = jnp.max(el, axis=-1, keepdims=True)
    i1 = jnp.min(jnp.where(el == v1, lane, big), axis=-1, keepdims=True)
    e1 = jnp.exp(v1 - v0)
    w0 = g_w / (1.0 + e1)
    w1 = g_w * e1 / (1.0 + e1)

    pos = ((i + tiles_per_seq - 1) % tiles_per_seq) * tm + lax.broadcasted_iota(I32, (tm, 1), 0)
    valid = (pos >= META0) & (i >= 1)
    onehot = jnp.where(((lane == i0) | (lane == i1)) & valid, 1.0, 0.0)
    rr = lax.broadcasted_iota(I32, (tm, tm), 0)
    cc = lax.broadcasted_iota(I32, (tm, tm), 1)
    lower = jnp.where(cc < rr, 1.0, 0.0).astype(BF16)
    rank = jnp.dot(lower, onehot.astype(BF16), preferred_element_type=F32) + cnt_sc[0:1, :]

    g_att = jnp.concatenate([ga0_ref[...], ga1_ref[...]], axis=1).astype(F32)
    g_conv = jnp.concatenate([gc0_ref[...], gc1_ref[...]], axis=1).astype(F32)
    merged = _sigmoid(g_att) * br_att + _sigmoid(g_conv) * br_conv
    r_new = r_ref[...] + jnp.dot(merged.astype(BF16), wo_ref[...], preferred_element_type=F32)
    ro_ref[...] = r_new
    rn_sc[...] = r_new

    rank0 = jnp.sum(jnp.where(lane == i0, rank, 0.0), axis=-1, keepdims=True)
    rank1 = jnp.sum(jnp.where(lane == i1, rank, 0.0), axis=-1, keepdims=True)
    cnt_sc[0:1, :] = cnt_sc[0:1, :] + jnp.sum(onehot, axis=0, keepdims=True)
    cnt_ref[...] = jnp.broadcast_to(cnt_sc[0:1, :], cnt_ref.shape)

    w0 = jnp.where(valid, w0, 0.0)
    w1 = jnp.where(valid, w1, 0.0)
    out = jnp.where(lane == 0, (i0 - first_e).astype(F32), 0.0)
    out = jnp.where(lane == 1, (i1 - first_e).astype(F32), out)
    out = jnp.where(lane == 2, w0, out)
    out = jnp.where(lane == 3, w1, out)
    out = jnp.where(lane == 4, rank0, out)
    out = jnp.where(lane == 5, rank1, out)
    route_ref[...] = out
    route_t_ref[...] = out.T[:ROUTE_FIELDS, :]


def _merge_route(att, cv, z, col_ga, col_gc, r, wao, wco, wo, gain, wr, br, tiles_per_seq):
    n_tok, d = r.shape
    aw = att.shape[1]
    cw = cv.shape[1]
    tm = SEQ_TILE
    half = d // 2
    gab, gcb = col_ga // half, col_gc // half
    const = lambda i: (0, 0)
    single = pl.Buffered(1)
    vmem = (2 * tm * (aw * 2 + cw * 2 + 2 * d * 2 + 3 * d * 4 + LANES * 4)
            + (aw * d + cw * d + d * d + d * LANES) * 2 + 8 * tm * d * 4)
    n_tiles = n_tok // tm
    cur = lambda i: jnp.minimum(i, n_tiles - 1)
    prev = lambda i: jnp.maximum(i - 1, 0)
    return pl.pallas_call(
        functools.partial(_merge_route_kernel, tiles_per_seq=tiles_per_seq),
        grid=(n_tiles + 1,),
        in_specs=[
            pl.BlockSpec((tm, aw), lambda i: (cur(i), 0)),
            pl.BlockSpec((tm, cw), lambda i: (cur(i), 0)),
            pl.BlockSpec((tm, half), lambda i: (cur(i), gab)),
            pl.BlockSpec((tm, half), lambda i: (cur(i), gab + 1)),
            pl.BlockSpec((tm, half), lambda i: (cur(i), gcb)),
            pl.BlockSpec((tm, half), lambda i: (cur(i), gcb + 1)),
            pl.BlockSpec((tm, d), lambda i: (cur(i), 0)),
            pl.BlockSpec(wao.shape, const, pipeline_mode=single),
            pl.BlockSpec(wco.shape, const, pipeline_mode=single),
            pl.BlockSpec(wo.shape, const, pipeline_mode=single),
            pl.BlockSpec((1, d), const),
            pl.BlockSpec(wr.shape, const, pipeline_mode=single),
            pl.BlockSpec((1, LANES), const),
        ],
        out_specs=[
            pl.BlockSpec((tm, d), lambda i: (cur(i), 0)),
            pl.BlockSpec((tm, d // 2), lambda i: (prev(i), 0)),
            pl.BlockSpec((tm, LANES), lambda i: (prev(i), 0)),
            pl.BlockSpec((ROUTE_FIELDS, tm), lambda i: (0, prev(i))),
            pl.BlockSpec((8, LANES), const),
        ],
        out_shape=[
            jax.ShapeDtypeStruct((n_tok, d), F32),
            jax.ShapeDtypeStruct((n_tok, d // 2), U32),
            jax.ShapeDtypeStruct((n_tok, LANES), F32),
            jax.ShapeDtypeStruct((ROUTE_FIELDS, n_tok), F32),
            jax.ShapeDtypeStruct((8, LANES), F32),
        ],
        scratch_shapes=[pltpu.VMEM((8, LANES), F32), pltpu.VMEM((tm, d), F32)],
        compiler_params=pltpu.CompilerParams(
            dimension_semantics=("arbitrary",), vmem_limit_bytes=_vmem_limit(vmem + tm * d * 4)),
        name="merge_route",
    )(att, cv, z, z, z, z, r, wao, wco, wo, gain, wr, br)


def _dispatch_kernel(dest_ref, zflag_ref, n_ref, xs_ref, zero_sc, stage, sem, zsem, *, tiles_per_seq):
    i = pl.program_id(0)
    tm = zero_sc.shape[0]
    n_tiles = xs_ref.shape[0] // tm
    n_tok = pl.num_programs(0) * tm

    def zero_copy(t):
        return pltpu.make_async_copy(zero_sc, xs_ref.at[pl.ds(pl.multiple_of(t * tm, tm), tm)], zsem)

    @pl.when(i == 0)
    def _():
        zero_sc[...] = jnp.zeros_like(zero_sc)

        def issue_zero(t, _):
            @pl.when(zflag_ref[t] != 0)
            def _():
                zero_copy(t).start()
            return 0

        def drain_zero(t, _):
            @pl.when(zflag_ref[t] != 0)
            def _():
                zero_copy(t).wait()
            return 0

        lax.fori_loop(0, n_tiles, issue_zero, 0)
        lax.fori_loop(0, n_tiles, drain_zero, 0)

    slot = i % 2
    last = pl.num_programs(0) - 1

    def first_row_group(step):
        return jnp.where(step % tiles_per_seq == 0, META0, 0) // SUBLANES

    def row_copy(step, buf, g, u, k):
        d = dest_ref[k * n_tok + step * tm + g * SUBLANES + u]
        return pltpu.make_async_copy(stage.at[buf, g, pl.ds(u, 1)], xs_ref.at[pl.ds(d, 1)], sem.at[buf])

    def drain(step, buf):
        one_row = pltpu.make_async_copy(stage.at[buf, 0, pl.ds(0, 1)], xs_ref.at[pl.ds(0, 1)], sem.at[buf])

        def body(g, _):
            for _u in range(2 * SUBLANES):
                one_row.wait()
            return 0
        lax.fori_loop(first_row_group(step), tm // SUBLANES, body, 0)

    @pl.when(i >= 2)
    def _():
        drain(i - 2, slot)

    stage[slot] = n_ref[...]

    def issue(g, _):
        for u in range(SUBLANES):
            row_copy(i, slot, g, u, 0).start()
            row_copy(i, slot, g, u, 1).start()
        return 0

    lax.fori_loop(first_row_group(i), tm // SUBLANES, issue, 0)

    @pl.when(i == last)
    def _():
        @pl.when(i >= 1)
        def _():
            drain(i - 1, 1 - slot)
        drain(i, slot)


def _dispatch(dest, zflag, n, n_rows, tiles_per_seq):
    n_tok, w = n.shape
    tm = MOE_TM
    groups = tm // SUBLANES
    return pl.pallas_call(
        functools.partial(_dispatch_kernel, tiles_per_seq=tiles_per_seq),
        grid_spec=pltpu.PrefetchScalarGridSpec(
            num_scalar_prefetch=2,
            grid=(n_tok // tm,),
            in_specs=[pl.BlockSpec((groups, SUBLANES, w), lambda i, *_: (i, 0, 0))],
            out_specs=pl.BlockSpec(memory_space=pl.ANY),
            scratch_shapes=[pltpu.VMEM((tm, w), n.dtype), pltpu.VMEM((2, groups, SUBLANES, w), n.dtype),
                            pltpu.SemaphoreType.DMA((2,)), pltpu.SemaphoreType.DMA],
        ),
        out_shape=jax.ShapeDtypeStruct((n_rows, w), n.dtype),
        compiler_params=pltpu.CompilerParams(dimension_semantics=("arbitrary",),
                                             has_side_effects=True),
        name="moe_dispatch",
    )(dest, zflag, n.reshape(n_tok // SUBLANES, SUBLANES, w))


def _expert_kernel(te_ref, first_ref, nxt_ref, na_ref, x_ref, wgu_hbm, wdn_hbm, y_ref,
                   gu_stage, dn_stage, gu_bf, dn_bf, sem, *, layer):
    i = pl.program_id(0)
    active = i < na_ref[0]

    def weight_copies(e):
        return (pltpu.make_async_copy(wgu_hbm.at[layer, e], gu_stage, sem.at[0]),
                pltpu.make_async_copy(wdn_hbm.at[layer, e], dn_stage, sem.at[1]))

    @pl.when(i == 0)
    def _():
        for cp in weight_copies(te_ref[0]):
            cp.start()

    @pl.when(active & (first_ref[i] != 0))
    def _():
        for cp in weight_copies(te_ref[i]):
            cp.wait()
        gu_bf[...] = gu_stage[...].astype(BF16)
        dn_bf[...] = dn_stage[...].astype(BF16)

        @pl.when(nxt_ref[i] >= 0)
        def _():
            for cp in weight_copies(nxt_ref[i]):
                cp.start()

    @pl.when(active)
    def _():
        de = dn_bf.shape[0]
        x = _unpack_rows(x_ref[...]).astype(BF16)
        h = jnp.dot(x, gu_bf[...], preferred_element_type=F32)
        a = h[:, :de]
        b = h[:, de:]
        act = a * _sigmoid(a) * b
        y_ref[...] = _pack_rows(jnp.dot(act.astype(BF16), dn_bf[...], preferred_element_type=F32))

    @pl.when(jnp.logical_not(active))
    def _():
        y_ref[...] = jnp.zeros_like(y_ref)


def _experts(tile_e, first, nxt, n_act, xs, w_gate_up, w_down, layer):
    n_rows, w = xs.shape
    d = w_gate_up.shape[2]
    assert d == 2 * w
    tm = MOE_TM
    de2 = w_gate_up.shape[3]
    de = w_down.shape[2]
    row_map = lambda i, te, fi, nx, na: (jnp.minimum(i, na[0] - 1), 0)
    vmem = (d * de2 + de * d) * (4 + 2) + 6 * tm * d * 4 + 6 * tm * de2 * 4
    return pl.pallas_call(
        functools.partial(_expert_kernel, layer=layer),
        grid_spec=pltpu.PrefetchScalarGridSpec(
            num_scalar_prefetch=4,
            grid=(n_rows // tm,),
            in_specs=[
                pl.BlockSpec((tm, w), row_map),
                pl.BlockSpec(memory_space=pl.ANY),
                pl.BlockSpec(memory_space=pl.ANY),
            ],
            out_specs=pl.BlockSpec((tm, w), lambda i, *_: (i, 0)),
            scratch_shapes=[
                pltpu.VMEM((d, de2), F32), pltpu.VMEM((de, d), F32),
                pltpu.VMEM((d, de2), BF16), pltpu.VMEM((de, d), BF16),
                pltpu.SemaphoreType.DMA((2,)),
            ],
        ),
        out_shape=jax.ShapeDtypeStruct((n_rows, w), xs.dtype),
        compiler_params=pltpu.CompilerParams(
            dimension_semantics=("arbitrary",), vmem_limit_bytes=_vmem_limit(vmem)),
        name="moe_experts",
    )(tile_e, first, nxt, n_act, xs, w_gate_up, w_down)


def _combine_kernel(dest_ref, r_ref, route_ref, y_ref, *rest, tile_of, final, n_tok):
    if final:
        g_ref, o_ref, ybuf, sem = rest
    else:
        o_ref, ybuf, sem = rest
    tm = r_ref.shape[0]
    i = pl.program_id(0)
    slot = i % 2

    def issue(step, buf):
        tile = tile_of(step)

        def body(g, _):
            for u in range(SUBLANES):
                for k in range(2):
                    d = dest_ref[k * n_tok + tile * tm + g * SUBLANES + u]
                    pltpu.make_async_copy(y_ref.at[pl.ds(d, 1)], ybuf.at[buf, k, g, pl.ds(u, 1)],
                                          sem.at[buf]).start()
            return 0
        lax.fori_loop(0, tm // SUBLANES, body, 0)

    @pl.when(i == 0)
    def _():
        issue(i, slot)

    @pl.when(i + 1 < pl.num_programs(0))
    def _():
        issue(i + 1, 1 - slot)

    one_row = pltpu.make_async_copy(y_ref.at[pl.ds(0, 1)], ybuf.at[slot, 0, 0, pl.ds(0, 1)], sem.at[slot])

    def drain(g, _):
        for _u in range(2 * SUBLANES):
            one_row.wait()
        return 0
    lax.fori_loop(0, tm // SUBLANES, drain, 0)

    route = route_ref[...]
    w = ybuf.shape[-1]
    out = (r_ref[...] + route[:, 2:3] * _unpack_rows(ybuf[slot, 0].reshape(tm, w))
           + route[:, 3:4] * _unpack_rows(ybuf[slot, 1].reshape(tm, w)))
    if final:
        ms = jnp.mean(out * out, axis=-1, keepdims=True)
        out = out * lax.rsqrt(ms + RMS_EPS) * g_ref[...]
    o_ref[...] = out


def _combine(dest, r, route, y, tiles_per_seq, final_gain=None):
    n_tok, d = r.shape
    tm = MOE_TM
    final = final_gain is not None
    if final:
        real_tiles = tiles_per_seq - P0 // tm
        tile_of = lambda i: (i // real_tiles) * tiles_per_seq + P0 // tm + i % real_tiles
        n_out_tiles = (n_tok // tm // tiles_per_seq) * real_tiles
    else:
        tile_of = lambda i: i
        n_out_tiles = n_tok // tm
    in_specs = [
        pl.BlockSpec((tm, d), lambda i, *_: (tile_of(i), 0)),
        pl.BlockSpec((tm, LANES), lambda i, *_: (tile_of(i), 0)),
        pl.BlockSpec(memory_space=pl.ANY),
    ]
    args = [dest, r, route, y]
    if final:
        in_specs.append(pl.BlockSpec((1, d), lambda i, *_: (0, 0)))
        args.append(final_gain)
    return pl.pallas_call(
        functools.partial(_combine_kernel, tile_of=tile_of, final=final, n_tok=n_tok),
        grid_spec=pltpu.PrefetchScalarGridSpec(
            num_scalar_prefetch=1,
            grid=(n_out_tiles,),
            in_specs=in_specs,
            out_specs=pl.BlockSpec((tm, d), lambda i, *_: (i, 0)),
            scratch_shapes=[pltpu.VMEM((2, 2, tm // SUBLANES, SUBLANES, d // 2), y.dtype),
                            pltpu.SemaphoreType.DMA((2,))],
        ),
        out_shape=jax.ShapeDtypeStruct((n_out_tiles * tm, d), F32),
        compiler_params=pltpu.CompilerParams(
            dimension_semantics=("arbitrary",),
            vmem_limit_bytes=_vmem_limit(8 * tm * d * 4 + 4 * tm * d * 4 + (8 << 20))),
        name="moe_combine_final" if final else "moe_combine",
    )(*args)


def _routing_tables(route_t, counts, n_tiles_max, valid_tok):
    tm = MOE_TM
    cnt = counts[0, ROUTE_FIRST_EXPERT_LANE:ROUTE_FIRST_EXPERT_LANE + N_EXPERTS].astype(I32)
    ntile = (cnt + tm - 1) // tm
    tile_end = jnp.cumsum(ntile)
    tile_start = tile_end - ntile
    n_act = tile_end[-1:]
    e_ids = route_t[0:2].astype(I32)
    rank = route_t[4:6].astype(I32)
    row_start = tile_start * tm
    base = jnp.zeros_like(e_ids)
    for e in range(N_EXPERTS):
        base = jnp.where(e_ids == e, row_start[e], base)
    dest = jnp.where(valid_tok[None, :], base + rank, 0).reshape(-1)
    all_tiles = jnp.arange(n_tiles_max, dtype=I32)
    clamped = jnp.minimum(all_tiles, n_act[0] - 1)
    tile_e = jnp.sum((clamped[:, None] >= tile_end[None, :]).astype(I32), axis=1)
    tile_e = jnp.minimum(tile_e, N_EXPERTS - 1)
    first = jnp.concatenate([jnp.ones((1,), I32), (tile_e[1:] != tile_e[:-1]).astype(I32)])
    end_of_mine = tile_end[tile_e]
    nxt = jnp.where(end_of_mine < n_act[0], tile_e[jnp.minimum(end_of_mine, n_tiles_max - 1)], -1)
    is_expert_tail = jnp.any((all_tiles[:, None] == tile_end[None, :] - 1) & (ntile[None, :] > 0), axis=1)
    zflag = (is_expert_tail | (all_tiles >= n_act[0])).astype(I32)
    return dest, tile_e, first, nxt.astype(I32), n_act.astype(I32), zflag


def kernel(x, meta, norm_mix, w_in, b_forget, w_attn_out, conv_w, conv_b, conv_ln_g, conv_ln_b,
           w_conv_out, w_out, norm_ffn, w_router_group, b_router_group, w_router_expert,
           b_router_expert, w_gate_up, w_down, norm_final):
    batch, seq, d = x.shape
    depth = w_in.shape[0]
    heads = b_forget.shape[1]
    att_w = heads * HEAD_DIM
    conv_c = conv_b.shape[1]
    lp = P0 + seq
    tiles_per_seq = lp // SEQ_TILE
    n_tok = batch * lp
    assert seq % SEQ_TILE == 0 and n_tok % INPROJ_TM == 0
    assert (3 * att_w) % INPROJ_TN == 0 and (2 * conv_c + 2 * d) % INPROJ_TN == 0 and d % 2 == 0
    assert att_w % LANES == 0 and conv_c % LANES == 0 and SEQ_TILE == MOE_TM

    r = jnp.pad(x, ((0, 0), (P0, 0), (0, 0)))
    r = lax.dynamic_update_slice(
        r, jnp.broadcast_to(meta.astype(x.dtype)[None], (batch, N_META, d)), (0, META0, 0))
    r = r.reshape(n_tok, d)

    c_f = 3 * att_w
    c_u = c_f + heads
    m_q, m_k, m_v = 0, att_w, 2 * att_w
    m_ua = 3 * att_w
    m_ub = m_ua + conv_c
    m_ga = m_ub + conv_c
    m_gc = m_ga + d

    valid_tok = jnp.tile(jnp.arange(lp) >= META0, batch)
    n_pairs = 2 * batch * (lp - META0)
    n_tiles_max = n_pairs // MOE_TM + N_EXPERTS
    n_rows = n_tiles_max * MOE_TM

    out = None
    for l in range(depth):
        wf = jnp.pad(w_in[l, :, c_f:c_u], ((0, 0), (0, LANES - heads))).astype(BF16)
        z, f = _inproj(r, norm_mix[l][None], w_in[l, :, :c_f].astype(BF16),
                       w_in[l, :, c_u:].astype(BF16), wf)
        cum = _forget_cumsum(f.reshape(batch, lp, LANES),
                             jnp.pad(b_forget[l], (0, LANES - heads))[None])
        z3 = z.reshape(batch, lp, -1)
        att = _attention(z3, cum, m_q, m_k, m_v).reshape(n_tok, att_w)
        conv_w_pad = jnp.concatenate([conv_w[l], jnp.zeros((1, conv_c), F32)], axis=0)
        cv = _conv_branch(z3, m_ua, m_ub, conv_w_pad, conv_b[l][None], conv_ln_g[l][None],
                          conv_ln_b[l][None]).reshape(n_tok, conv_c)

        wr = jnp.zeros((d, LANES), F32)
        wr = wr.at[:, :N_GROUPS].set(w_router_group[l])
        wr = wr.at[:, N_GROUPS:N_GROUPS + N_EXPERTS].set(w_router_expert[l]).astype(BF16)
        br = jnp.zeros((1, LANES), F32)
        br = br.at[0, :N_GROUPS].set(b_router_group[l])
        br = br.at[0, N_GROUPS:N_GROUPS + N_EXPERTS].set(b_router_expert[l])
        r, n2, route, route_t, counts = _merge_route(
            att, cv, z, m_ga, m_gc, r, w_attn_out[l].astype(BF16), w_conv_out[l].astype(BF16),
            w_out[l].astype(BF16), norm_ffn[l][None], wr, br, tiles_per_seq)

        dest, tile_e, first, nxt, n_act, zflag = _routing_tables(route_t, counts, n_tiles_max, valid_tok)
        xs = _dispatch(dest, zflag, n2, n_rows, tiles_per_seq)
        y = _experts(tile_e, first, nxt, n_act, xs, w_gate_up, w_down, l)
        if l + 1 < depth:
            r = _combine(dest, r, route, y, tiles_per_seq)
        else:
            out = _combine(dest, r, route, y, tiles_per_seq, final_gain=norm_final[None])
    return out.reshape(batch, seq, d)
```

```python
import functools

import jax
import jax.numpy as jnp
from jax import lax
from jax.experimental import pallas as pl
from jax.experimental.pallas import tpu as pltpu

F32 = jnp.float32
BF16 = jnp.bfloat16
I32 = jnp.int32
U32 = jnp.uint32

LANES = 128
SUBLANES = 8
MXU_DIM = 256
VMEM_BYTES_V7X = 64 * 1024 * 1024

N_META = 16
HEAD_DIM = 64
HEADS_PER_BLOCK = LANES // HEAD_DIM
ATT_HEADS = 16
ATT_BLOCK = ATT_HEADS * HEAD_DIM
CONV_K = 31
N_GROUPS = 4
EXPERTS_PER_GROUP = 8
N_EXPERTS = N_GROUPS * EXPERTS_PER_GROUP
LOG2_E = 1.4426950408889634
RMS_EPS = 1e-6
LN_EPS = 1e-5

SEQ_TILE = 256
P0 = SEQ_TILE
META0 = P0 - N_META
INPROJ_TM = 1024
INPROJ_TN = 1024
MOE_TM = 256
CONV_HALO = 32
DMA_UNROLL = 8
ROUTE_FIRST_EXPERT_LANE = N_GROUPS
ROUTE_FIELDS = 8
MASK_VALUE = -1e30


def _vmem_limit(nbytes):
    return int(min(max(nbytes, 16 * 1024 * 1024), VMEM_BYTES_V7X - 6 * 1024 * 1024))


def _sigmoid(x):
    return 1.0 / (1.0 + jnp.exp(-x))


def _pack_rows(v):
    c = v.shape[1] // 2
    lo = lax.bitcast_convert_type(v[:, :c].astype(BF16).astype(F32), U32)
    hi = lax.bitcast_convert_type(v[:, c:].astype(BF16).astype(F32), U32)
    return (hi & jnp.uint32(0xFFFF0000)) | (lo >> 16)


def _unpack_rows(w):
    lo = lax.bitcast_convert_type(w << 16, F32)
    hi = lax.bitcast_convert_type(w & jnp.uint32(0xFFFF0000), F32)
    return jnp.concatenate([lo, hi], axis=1)


def _inproj_kernel(r_ref, g_ref, wa_ref, wb_ref, wf_ref, z_ref, f_ref, n_sc, *, na):
    j = pl.program_id(1)

    @pl.when(j == 0)
    def _():
        x = r_ref[...]
        ms = jnp.mean(x * x, axis=-1, keepdims=True)
        n = (x * lax.rsqrt(ms + RMS_EPS) * g_ref[...]).astype(BF16)
        n_sc[...] = n
        f_ref[...] = jnp.dot(n, wf_ref[...], preferred_element_type=F32)

    @pl.when(j < na)
    def _():
        z_ref[...] = jnp.dot(n_sc[...], wa_ref[...], preferred_element_type=F32).astype(z_ref.dtype)

    @pl.when(j >= na)
    def _():
        z_ref[...] = jnp.dot(n_sc[...], wb_ref[...], preferred_element_type=F32).astype(z_ref.dtype)


def _inproj(r, gain, w_a, w_b, wf):
    n_tok, d = r.shape
    tm, tn = INPROJ_TM, INPROJ_TN
    na, nb = w_a.shape[1] // tn, w_b.shape[1] // tn
    cols = (na + nb) * tn
    vmem = 2 * tm * d * 4 + 4 * d * tn * 2 + 2 * tm * tn * 2 + tm * d * 2 + 2 * (d * 2 + tm * 4) * LANES
    return pl.pallas_call(
        functools.partial(_inproj_kernel, na=na),
        grid=(n_tok // tm, na + nb),
        in_specs=[
            pl.BlockSpec((tm, d), lambda i, j: (i, 0)),
            pl.BlockSpec((1, d), lambda i, j: (0, 0)),
            pl.BlockSpec((d, tn), lambda i, j: (0, jnp.minimum(j, na - 1))),
            pl.BlockSpec((d, tn), lambda i, j: (0, jnp.maximum(j - na, 0))),
            pl.BlockSpec((d, LANES), lambda i, j: (0, 0)),
        ],
        out_specs=[
            pl.BlockSpec((tm, tn), lambda i, j: (i, j)),
            pl.BlockSpec((tm, LANES), lambda i, j: (i, 0)),
        ],
        out_shape=[
            jax.ShapeDtypeStruct((n_tok, cols), BF16),
            jax.ShapeDtypeStruct((n_tok, LANES), F32),
        ],
        scratch_shapes=[pltpu.VMEM((tm, d), BF16)],
        compiler_params=pltpu.CompilerParams(
            dimension_semantics=("parallel", "arbitrary"),
            vmem_limit_bytes=_vmem_limit(vmem + (8 << 20))),
        name="inproj",
    )(r, gain, w_a, w_b, wf)


def _cum_kernel(f_ref, b_ref, o_ref):
    ch = MXU_DIM
    lp = f_ref.shape[1]
    row = lax.broadcasted_iota(I32, (ch, ch), 0)
    col = lax.broadcasted_iota(I32, (ch, ch), 1)
    lower = (col <= row).astype(F32)
    carry = jnp.zeros((1, LANES), F32)
    for c in range(lp // ch):
        x = f_ref[0, c * ch:(c + 1) * ch, :] + b_ref[...]
        log_f = jnp.minimum(x, 0.0) - jnp.log1p(jnp.exp(-jnp.abs(x)))
        loc = jnp.dot(lower, log_f, preferred_element_type=F32,
                      precision=lax.Precision.HIGHEST) + carry
        o_ref[0, c * ch:(c + 1) * ch, :] = loc
        carry = loc[ch - 1:ch, :]


def _forget_cumsum(f3, b_forget):
    batch, lp, _ = f3.shape
    return pl.pallas_call(
        _cum_kernel,
        grid=(batch,),
        in_specs=[
            pl.BlockSpec((1, lp, LANES), lambda b: (b, 0, 0)),
            pl.BlockSpec((1, LANES), lambda b: (0, 0)),
        ],
        out_specs=pl.BlockSpec((1, lp, LANES), lambda b: (b, 0, 0)),
        out_shape=jax.ShapeDtypeStruct((batch, lp, LANES), F32),
        compiler_params=pltpu.CompilerParams(dimension_semantics=("parallel",)),
        name="forget_cumsum",
    )(f3, b_forget)


def _attn_kernel(q_ref, k_ref, v_ref, c_ref, o_ref, kaug_sc, vt_sc, qaug_sc, s_a, s_b,
                 m_sc, l_sc, acc_sc):
    tq = q_ref.shape[1]
    tk = tq
    lp = k_ref.shape[1]
    hg = pl.program_id(1)
    qi = pl.program_id(2)
    lane = lax.broadcasted_iota(I32, (1, LANES), 1)
    nt_dims = (((1,), (1,)), ((), ()))

    def own_lanes(h):
        return lane < HEAD_DIM if h % HEADS_PER_BLOCK == 0 else lane >= HEAD_DIM

    def bias_lane(h):
        return HEAD_DIM if h % HEADS_PER_BLOCK == 0 else 0

    def block_lanes(h):
        blk = h // HEADS_PER_BLOCK
        return slice(blk * LANES, (blk + 1) * LANES)

    @pl.when(qi == 0)
    def _():
        for c in range(lp // tk):
            rows = slice(c * tk, (c + 1) * tk)
            vt_sc[:, rows] = v_ref[0, rows, :].astype(F32).T.astype(BF16)
            cum = c_ref[0, rows, :]
            key_pos = c * tk + lax.broadcasted_iota(I32, (tk, 1), 0)
            for h in range(ATT_HEADS):
                head = hg * ATT_HEADS + h
                col = jnp.sum(jnp.where(lane == head, cum, 0.0), axis=-1, keepdims=True) * LOG2_E
                hi = col.astype(BF16).astype(F32)
                rem = col - hi
                mid = rem.astype(BF16).astype(F32)
                low = rem - mid
                hi = jnp.where(key_pos < META0, -MASK_VALUE, hi)
                a = bias_lane(h)
                bias = jnp.where(lane == a, hi, jnp.where(lane == a + 1, mid,
                                                          jnp.where(lane == a + 2, low, 0.0)))
                kaug_sc[h, rows, :] = jnp.where(own_lanes(h), k_ref[0, rows, block_lanes(h)],
                                                bias.astype(BF16))

    for h in range(ATT_HEADS):
        a = bias_lane(h)
        minus_one = jnp.where((lane >= a) & (lane < a + 3), -1.0, 0.0)
        scaled = q_ref[0, :, block_lanes(h)].astype(F32) * (HEAD_DIM ** -0.5 * LOG2_E)
        qaug_sc[h] = jnp.where(own_lanes(h), scaled, minus_one).astype(BF16)
        m_sc[h] = jnp.full((1, tq), MASK_VALUE, F32)
        l_sc[h] = jnp.zeros((1, tq), F32)
        acc_sc[h] = jnp.zeros((HEAD_DIM, tq), F32)

    def scores_into(dst, j):
        s0 = pl.multiple_of(j * tk, tk)
        for h in range(ATT_HEADS):
            dst[h] = lax.dot_general(kaug_sc[h, pl.ds(s0, tk), :], qaug_sc[h], nt_dims,
                                     preferred_element_type=F32)

    def softmax_pv(src, j, diagonal):
        s0 = pl.multiple_of(j * tk, tk)
        for h in range(ATT_HEADS):
            st = src[h]
            if diagonal:
                key = lax.broadcasted_iota(I32, (tk, tq), 0)
                qry = lax.broadcasted_iota(I32, (tk, tq), 1)
                st = jnp.where(key <= qry, st, MASK_VALUE)
            m = m_sc[h]
            m_new = jnp.maximum(m, jnp.max(st, axis=0, keepdims=True))
            alpha = jnp.exp2(m - m_new)
            p = jnp.exp2(st - m_new)
            l_sc[h] = alpha * l_sc[h] + jnp.sum(p, axis=0, keepdims=True)
            pv = jnp.dot(vt_sc[h * HEAD_DIM:(h + 1) * HEAD_DIM, pl.ds(s0, tk)], p.astype(BF16),
                         preferred_element_type=F32)
            acc_sc[h] = alpha * acc_sc[h] + pv
            m_sc[h] = m_new

    scores_into(s_a, 0)

    def pair(i, _):
        j = 2 * i
        scores_into(s_b, j + 1)
        softmax_pv(s_a, j, False)
        scores_into(s_a, j + 2)
        softmax_pv(s_b, j + 1, False)
        return 0

    lax.fori_loop(0, qi // 2, pair, 0)

    @pl.when(qi % 2 == 0)
    def _():
        softmax_pv(s_a, qi, True)

    @pl.when(qi % 2 == 1)
    def _():
        scores_into(s_b, qi)
        softmax_pv(s_a, qi - 1, False)
        softmax_pv(s_b, qi, True)

    att_t = jnp.concatenate([acc_sc[h] / l_sc[h] for h in range(ATT_HEADS)], axis=0)
    o_ref[0] = att_t.T.astype(o_ref.dtype)


def _attention(z3, cum, col_q, col_k, col_v):
    batch, lp, _ = z3.shape
    att_w = col_k - col_q
    tq = SEQ_TILE
    qb, kb, vb = col_q // ATT_BLOCK, col_k // ATT_BLOCK, col_v // ATT_BLOCK
    vmem = (2 * (2 * lp + 2 * tq) * ATT_BLOCK * 2 + 2 * lp * LANES * 4
            + (ATT_HEADS * lp * LANES + ATT_BLOCK * lp + ATT_HEADS * tq * LANES) * 2
            + (2 * ATT_HEADS * tq * tq + ATT_HEADS * (HEAD_DIM + 2 * SUBLANES) * tq) * 4
            + (8 << 20))
    return pl.pallas_call(
        _attn_kernel,
        grid=(batch, att_w // ATT_BLOCK, lp // tq),
        in_specs=[
            pl.BlockSpec((1, tq, ATT_BLOCK), lambda b, h, i: (b, i, qb + h)),
            pl.BlockSpec((1, lp, ATT_BLOCK), lambda b, h, i: (b, 0, kb + h)),
            pl.BlockSpec((1, lp, ATT_BLOCK), lambda b, h, i: (b, 0, vb + h)),
            pl.BlockSpec((1, lp, LANES), lambda b, h, i: (b, 0, 0)),
        ],
        out_specs=pl.BlockSpec((1, tq, ATT_BLOCK), lambda b, h, i: (b, i, h)),
        out_shape=jax.ShapeDtypeStruct((batch, lp, att_w), BF16),
        scratch_shapes=[
            pltpu.VMEM((ATT_HEADS, lp, LANES), BF16), pltpu.VMEM((ATT_BLOCK, lp), BF16),
            pltpu.VMEM((ATT_HEADS, tq, LANES), BF16),
            pltpu.VMEM((ATT_HEADS, tq, tq), F32), pltpu.VMEM((ATT_HEADS, tq, tq), F32),
            pltpu.VMEM((ATT_HEADS, 1, tq), F32), pltpu.VMEM((ATT_HEADS, 1, tq), F32),
            pltpu.VMEM((ATT_HEADS, HEAD_DIM, tq), F32),
        ],
        compiler_params=pltpu.CompilerParams(
            dimension_semantics=("parallel", "parallel", "arbitrary"),
            vmem_limit_bytes=_vmem_limit(vmem)),
        name="fox_attention",
    )(z3, z3, z3, cum)


def _conv_kernel(a_ref, b_ref, w_ref, cb_ref, g_ref, lb_ref, o_ref, zbuf, ybuf, zshift):
    tt = a_ref.shape[1]
    ch = a_ref.shape[2]
    t = pl.program_id(1)

    @pl.when(t == 0)
    def _():
        zbuf[0:CONV_HALO, :] = jnp.zeros((CONV_HALO, ch), F32)

    z = a_ref[0].astype(F32) * _sigmoid(b_ref[0].astype(F32))
    rows = t * tt + lax.broadcasted_iota(I32, (tt, 1), 0)
    zbuf[CONV_HALO:CONV_HALO + tt, :] = jnp.where(rows >= META0, z, 0.0)

    first = CONV_HALO - (CONV_K - 1)

    def chan_block(cb, _):
        c0 = pl.multiple_of(cb * LANES, LANES)
        for res in range(SUBLANES):
            span = tt + ((CONV_K - 1 - res) // SUBLANES) * SUBLANES
            zshift[res, 0:span, :] = zbuf[first + res:first + res + span, pl.ds(c0, LANES)]
        acc = jnp.zeros((tt, LANES), F32) + cb_ref[:, pl.ds(c0, LANES)]
        for k in range(CONV_K):
            base = (k // SUBLANES) * SUBLANES
            acc = acc + w_ref[k:k + 1, pl.ds(c0, LANES)] * zshift[k % SUBLANES, base:base + tt, :]
        ybuf[:, pl.ds(c0, LANES)] = acc
        return 0

    lax.fori_loop(0, ch // LANES, chan_block, 0)

    y = ybuf[...]
    mu = jnp.mean(y, axis=-1, keepdims=True)
    yc = y - mu
    var = jnp.mean(yc * yc, axis=-1, keepdims=True)
    zn = yc * lax.rsqrt(var + LN_EPS) * g_ref[...] + lb_ref[...]
    o_ref[0] = (zn * _sigmoid(zn)).astype(o_ref.dtype)
    zbuf[0:CONV_HALO, :] = zbuf[tt:tt + CONV_HALO, :]


def _conv_branch(z3, col_a, col_b, conv_w, conv_b, ln_g, ln_b):
    batch, lp, _ = z3.shape
    ch = conv_b.shape[1]
    tt = SEQ_TILE
    ab, bb = col_a // ch, col_b // ch
    return pl.pallas_call(
        _conv_kernel,
        grid=(batch, lp // tt),
        in_specs=[
            pl.BlockSpec((1, tt, ch), lambda b, t: (b, t, ab)),
            pl.BlockSpec((1, tt, ch), lambda b, t: (b, t, bb)),
            pl.BlockSpec(conv_w.shape, lambda b, t: (0, 0)),
            pl.BlockSpec((1, ch), lambda b, t: (0, 0)),
            pl.BlockSpec((1, ch), lambda b, t: (0, 0)),
            pl.BlockSpec((1, ch), lambda b, t: (0, 0)),
        ],
        out_specs=pl.BlockSpec((1, tt, ch), lambda b, t: (b, t, 0)),
        out_shape=jax.ShapeDtypeStruct((batch, lp, ch), BF16),
        scratch_shapes=[
            pltpu.VMEM((tt + CONV_HALO, ch), F32), pltpu.VMEM((tt, ch), F32),
            pltpu.VMEM((SUBLANES, tt + ((CONV_K - 1) // SUBLANES) * SUBLANES, LANES), F32),
        ],
        compiler_params=pltpu.CompilerParams(dimension_semantics=("parallel", "arbitrary")),
        name="conv_branch",
    )(z3, z3, conv_w, conv_b, ln_g, ln_b)


def _merge_route_kernel(att_ref, cv_ref, ga0_ref, ga1_ref, gc0_ref, gc1_ref, r_ref, wao_ref, wco_ref,
                        wo_ref, g_ref, wr_ref, br_ref, ro_ref, n_ref, route_ref, route_t_ref, cnt_ref,
                        cnt_sc, rn_sc, *, tiles_per_seq):
    i = pl.program_id(0)
    tm = r_ref.shape[0]

    @pl.when(i == 0)
    def _():
        cnt_sc[...] = jnp.zeros_like(cnt_sc)
        rn_sc[...] = jnp.zeros_like(rn_sc)

    br_att = jnp.dot(att_ref[...], wao_ref[...], preferred_element_type=F32)

    r_prev = rn_sc[...]
    ms = jnp.mean(r_prev * r_prev, axis=-1, keepdims=True)
    n = r_prev * lax.rsqrt(ms + RMS_EPS) * g_ref[...]
    n_ref[...] = _pack_rows(n).reshape(n_ref.shape)
    logits = jnp.dot(n.astype(BF16), wr_ref[...], preferred_element_type=F32) + br_ref[...]

    br_conv = jnp.dot(cv_ref[...], wco_ref[...], preferred_element_type=F32)
    lane = lax.broadcasted_iota(I32, logits.shape, 1)
    big = jnp.int32(4 * LANES)
    first_e = ROUTE_FIRST_EXPERT_LANE

    gl = jnp.where(lane < N_GROUPS, logits, -jnp.inf)
    gmax = jnp.max(gl, axis=-1, keepdims=True)
    gsum = jnp.sum(jnp.exp(gl - gmax), axis=-1, keepdims=True)
    g_w = 1.0 / gsum
    g_idx = jnp.min(jnp.where(gl == gmax, lane, big), axis=-1, keepdims=True)

    in_group = (lane >= first_e + g_idx * EXPERTS_PER_GROUP) & \
               (lane < first_e + (g_idx + 1) * EXPERTS_PER_GROUP)
    el = jnp.where(in_group, logits, -jnp.inf)
    v0 = jnp.max(el, axis=-1, keepdims=True)
    i0 = jnp.min(jnp.where(el == v0, lane, big), axis=-1, keepdims=True)
    el = jnp.where(lane == i0, -jnp.inf, el)
    v1 = jnp.max(el, axis=-1, keepdims=True)
    i1 = jnp.min(jnp.where(el == v1, lane, big), axis=-1, keepdims=True)
    e1 = jnp.exp(v1 - v0)
    w0 = g_w / (1.0 + e1)
    w1 = g_w * e1 / (1.0 + e1)

    pos = ((i + tiles_per_seq - 1) % tiles_per_seq) * tm + lax.broadcasted_iota(I32, (tm, 1), 0)
    valid = (pos >= META0) & (i >= 1)
    onehot = jnp.where(((lane == i0) | (lane == i1)) & valid, 1.0, 0.0)
    rr = lax.broadcasted_iota(I32, (tm, tm), 0)
    cc = lax.broadcasted_iota(I32, (tm, tm), 1)
    lower = jnp.where(cc < rr, 1.0, 0.0).astype(BF16)
    rank = jnp.dot(lower, onehot.astype(BF16), preferred_element_type=F32) + cnt_sc[0:1, :]

    g_att = jnp.concatenate([ga0_ref[...], ga1_ref[...]], axis=1).astype(F32)
    g_conv = jnp.concatenate([gc0_ref[...], gc1_ref[...]], axis=1).astype(F32)
    merged = _sigmoid(g_att) * br_att + _sigmoid(g_conv) * br_conv
    r_new = r_ref[...] + jnp.dot(merged.astype(BF16), wo_ref[...], preferred_element_type=F32)
    ro_ref[...] = r_new
    rn_sc[...] = r_new

    rank0 = jnp.sum(jnp.where(lane == i0, rank, 0.0), axis=-1, keepdims=True)
    rank1 = jnp.sum(jnp.where(lane == i1, rank, 0.0), axis=-1, keepdims=True)
    cnt_sc[0:1, :] = cnt_sc[0:1, :] + jnp.sum(onehot, axis=0, keepdims=True)
    cnt_ref[...] = jnp.broadcast_to(cnt_sc[0:1, :], cnt_ref.shape)

    w0 = jnp.where(valid, w0, 0.0)
    w1 = jnp.where(valid, w1, 0.0)
    out = jnp.where(lane == 0, (i0 - first_e).astype(F32), 0.0)
    out = jnp.where(lane == 1, (i1 - first_e).astype(F32), out)
    out = jnp.where(lane == 2, w0, out)
    out = jnp.where(lane == 3, w1, out)
    out = jnp.where(lane == 4, rank0, out)
    out = jnp.where(lane == 5, rank1, out)
    route_ref[...] = out
    route_t_ref[...] = out.T[:ROUTE_FIELDS, :]


def _merge_route(att, cv, z, col_ga, col_gc, r, wao, wco, wo, gain, wr, br, tiles_per_seq):
    n_tok, d = r.shape
    aw = att.shape[1]
    cw = cv.shape[1]
    tm = SEQ_TILE
    half = d // 2
    gab, gcb = col_ga // half, col_gc // half
    const = lambda i: (0, 0)
    single = pl.Buffered(1)
    vmem = (2 * tm * (aw * 2 + cw * 2 + 2 * d * 2 + 3 * d * 4 + LANES * 4)
            + (aw * d + cw * d + d * d + d * LANES) * 2 + 8 * tm * d * 4)
    n_tiles = n_tok // tm
    cur = lambda i: jnp.minimum(i, n_tiles - 1)
    prev = lambda i: jnp.maximum(i - 1, 0)
    return pl.pallas_call(
        functools.partial(_merge_route_kernel, tiles_per_seq=tiles_per_seq),
        grid=(n_tiles + 1,),
        in_specs=[
            pl.BlockSpec((tm, aw), lambda i: (cur(i), 0)),
            pl.BlockSpec((tm, cw), lambda i: (cur(i), 0)),
            pl.BlockSpec((tm, half), lambda i: (cur(i), gab)),
            pl.BlockSpec((tm, half), lambda i: (cur(i), gab + 1)),
            pl.BlockSpec((tm, half), lambda i: (cur(i), gcb)),
            pl.BlockSpec((tm, half), lambda i: (cur(i), gcb + 1)),
            pl.BlockSpec((tm, d), lambda i: (cur(i), 0)),
            pl.BlockSpec(wao.shape, const, pipeline_mode=single),
            pl.BlockSpec(wco.shape, const, pipeline_mode=single),
            pl.BlockSpec(wo.shape, const, pipeline_mode=single),
            pl.BlockSpec((1, d), const),
            pl.BlockSpec(wr.shape, const, pipeline_mode=single),
            pl.BlockSpec((1, LANES), const),
        ],
        out_specs=[
            pl.BlockSpec((tm, d), lambda i: (cur(i), 0)),
            pl.BlockSpec((tm, d // 2 // LANES, LANES), lambda i: (prev(i), 0, 0)),
            pl.BlockSpec((tm, LANES), lambda i: (prev(i), 0)),
            pl.BlockSpec((ROUTE_FIELDS, tm), lambda i: (0, prev(i))),
            pl.BlockSpec((8, LANES), const),
        ],
        out_shape=[
            jax.ShapeDtypeStruct((n_tok, d), F32),
            jax.ShapeDtypeStruct((n_tok, d // 2 // LANES, LANES), U32),
            jax.ShapeDtypeStruct((n_tok, LANES), F32),
            jax.ShapeDtypeStruct((ROUTE_FIELDS, n_tok), F32),
            jax.ShapeDtypeStruct((8, LANES), F32),
        ],
        scratch_shapes=[pltpu.VMEM((8, LANES), F32), pltpu.VMEM((tm, d), F32)],
        compiler_params=pltpu.CompilerParams(
            dimension_semantics=("arbitrary",), vmem_limit_bytes=_vmem_limit(vmem + tm * d * 4)),
        name="merge_route",
    )(att, cv, z, z, z, z, r, wao, wco, wo, gain, wr, br)


def _dispatch_kernel(dest_ref, zflag_ref, n_ref, xs_ref, zero_sc, stage, sem, zsem, *, tiles_per_seq):
    i = pl.program_id(0)
    tm = zero_sc.shape[0]
    n_tiles = xs_ref.shape[0] // tm
    n_tok = pl.num_programs(0) * tm

    def zero_copy(t):
        return pltpu.make_async_copy(zero_sc, xs_ref.at[pl.ds(pl.multiple_of(t * tm, tm), tm)], zsem)

    @pl.when(i == 0)
    def _():
        zero_sc[...] = jnp.zeros_like(zero_sc)

        def issue_zero(t, _):
            @pl.when(zflag_ref[t] != 0)
            def _():
                zero_copy(t).start()
            return 0

        def drain_zero(t, _):
            @pl.when(zflag_ref[t] != 0)
            def _():
                zero_copy(t).wait()
            return 0

        lax.fori_loop(0, n_tiles, issue_zero, 0)
        lax.fori_loop(0, n_tiles, drain_zero, 0)

    slot = i % 2
    last = pl.num_programs(0) - 1

    def first_row_group(step):
        return jnp.where(step % tiles_per_seq == 0, META0, 0) // SUBLANES

    def row_copy(step, buf, g, u, k):
        t = g * SUBLANES + u
        d = dest_ref[k * n_tok + step * tm + t]
        return pltpu.make_async_copy(stage.at[buf, pl.ds(t, 1)], xs_ref.at[pl.ds(d, 1)], sem.at[buf])

    def drain(step, buf):
        one_row = pltpu.make_async_copy(stage.at[buf, pl.ds(0, 1)], xs_ref.at[pl.ds(0, 1)], sem.at[buf])

        def body(g, _):
            for _u in range(2 * SUBLANES):
                one_row.wait()
            return 0
        lax.fori_loop(first_row_group(step), tm // SUBLANES, body, 0)

    @pl.when(i >= 2)
    def _():
        drain(i - 2, slot)

    stage[slot] = n_ref[...]

    def issue(g, _):
        for u in range(SUBLANES):
            row_copy(i, slot, g, u, 0).start()
            row_copy(i, slot, g, u, 1).start()
        return 0

    lax.fori_loop(first_row_group(i), tm // SUBLANES, issue, 0)

    @pl.when(i == last)
    def _():
        @pl.when(i >= 1)
        def _():
            drain(i - 1, 1 - slot)
        drain(i, slot)


def _dispatch(dest, zflag, n, n_rows, tiles_per_seq):
    n_tok = n.shape[0]
    row = n.shape[1:]
    tm = MOE_TM
    return pl.pallas_call(
        functools.partial(_dispatch_kernel, tiles_per_seq=tiles_per_seq),
        grid_spec=pltpu.PrefetchScalarGridSpec(
            num_scalar_prefetch=2,
            grid=(n_tok // tm,),
            in_specs=[pl.BlockSpec((tm,) + row, lambda i, *_: (i, 0, 0))],
            out_specs=pl.BlockSpec(memory_space=pl.ANY),
            scratch_shapes=[pltpu.VMEM((tm,) + row, n.dtype), pltpu.VMEM((2, tm) + row, n.dtype),
                            pltpu.SemaphoreType.DMA((2,)), pltpu.SemaphoreType.DMA],
        ),
        out_shape=jax.ShapeDtypeStruct((n_rows,) + row, n.dtype),
        compiler_params=pltpu.CompilerParams(dimension_semantics=("arbitrary",),
                                             has_side_effects=True),
        name="moe_dispatch",
    )(dest, zflag, n)


def _expert_kernel(te_ref, first_ref, nxt_ref, na_ref, x_ref, wgu_hbm, wdn_hbm, y_ref,
                   gu_stage, dn_stage, gu_bf, dn_bf, sem, *, layer):
    i = pl.program_id(0)
    active = i < na_ref[0]

    def weight_copies(e):
        return (pltpu.make_async_copy(wgu_hbm.at[layer, e], gu_stage, sem.at[0]),
                pltpu.make_async_copy(wdn_hbm.at[layer, e], dn_stage, sem.at[1]))

    @pl.when(i == 0)
    def _():
        for cp in weight_copies(te_ref[0]):
            cp.start()

    @pl.when(active & (first_ref[i] != 0))
    def _():
        for cp in weight_copies(te_ref[i]):
            cp.wait()
        gu_bf[...] = gu_stage[...].astype(BF16)
        dn_bf[...] = dn_stage[...].astype(BF16)

        @pl.when(nxt_ref[i] >= 0)
        def _():
            for cp in weight_copies(nxt_ref[i]):
                cp.start()

    @pl.when(active)
    def _():
        de = dn_bf.shape[0]
        tm = x_ref.shape[0]
        x = _unpack_rows(x_ref[...].reshape(tm, dn_bf.shape[1] // 2)).astype(BF16)
        h = jnp.dot(x, gu_bf[...], preferred_element_type=F32)
        a = h[:, :de]
        b = h[:, de:]
        act = a * _sigmoid(a) * b
        y = _pack_rows(jnp.dot(act.astype(BF16), dn_bf[...], preferred_element_type=F32))
        y_ref[...] = y.reshape(y_ref.shape)

    @pl.when(jnp.logical_not(active))
    def _():
        y_ref[...] = jnp.zeros_like(y_ref)


def _experts(tile_e, first, nxt, n_act, xs, w_gate_up, w_down, layer):
    n_rows = xs.shape[0]
    row = xs.shape[1:]
    w = row[0] * row[1]
    d = w_gate_up.shape[2]
    assert d == 2 * w and row[1] == LANES
    tm = MOE_TM
    de2 = w_gate_up.shape[3]
    de = w_down.shape[2]
    row_map = lambda i, te, fi, nx, na: (jnp.minimum(i, na[0] - 1), 0, 0)
    vmem = (d * de2 + de * d) * (4 + 2) + 6 * tm * d * 4 + 6 * tm * de2 * 4
    return pl.pallas_call(
        functools.partial(_expert_kernel, layer=layer),
        grid_spec=pltpu.PrefetchScalarGridSpec(
            num_scalar_prefetch=4,
            grid=(n_rows // tm,),
            in_specs=[
                pl.BlockSpec((tm,) + row, row_map),
                pl.BlockSpec(memory_space=pl.ANY),
                pl.BlockSpec(memory_space=pl.ANY),
            ],
            out_specs=pl.BlockSpec((tm,) + row, lambda i, *_: (i, 0, 0)),
            scratch_shapes=[
                pltpu.VMEM((d, de2), F32), pltpu.VMEM((de, d), F32),
                pltpu.VMEM((d, de2), BF16), pltpu.VMEM((de, d), BF16),
                pltpu.SemaphoreType.DMA((2,)),
            ],
        ),
        out_shape=jax.ShapeDtypeStruct(xs.shape, xs.dtype),
        compiler_params=pltpu.CompilerParams(
            dimension_semantics=("arbitrary",), vmem_limit_bytes=_vmem_limit(vmem)),
        name="moe_experts",
    )(tile_e, first, nxt, n_act, xs, w_gate_up, w_down)


def _combine_kernel(dest_ref, r_ref, route_ref, y_ref, *rest, tile_of, final, n_tok):
    if final:
        g_ref, o_ref, ybuf, sem = rest
    else:
        o_ref, ybuf, sem = rest
    tm = r_ref.shape[0]
    i = pl.program_id(0)
    slot = i % 2

    def issue(step, buf):
        tile = tile_of(step)

        def body(g, _):
            for u in range(SUBLANES):
                t = g * SUBLANES + u
                for k in range(2):
                    d = dest_ref[k * n_tok + tile * tm + t]
                    pltpu.make_async_copy(y_ref.at[pl.ds(d, 1)], ybuf.at[buf, k, pl.ds(t, 1)],
                                          sem.at[buf]).start()
            return 0
        lax.fori_loop(0, tm // SUBLANES, body, 0)

    @pl.when(i == 0)
    def _():
        issue(i, slot)

    @pl.when(i + 1 < pl.num_programs(0))
    def _():
        issue(i + 1, 1 - slot)

    one_row = pltpu.make_async_copy(y_ref.at[pl.ds(0, 1)], ybuf.at[slot, 0, pl.ds(0, 1)], sem.at[slot])

    def drain(g, _):
        for _u in range(2 * SUBLANES):
            one_row.wait()
        return 0
    lax.fori_loop(0, tm // SUBLANES, drain, 0)

    route = route_ref[...]
    w = ybuf.shape[-2] * ybuf.shape[-1]
    out = (r_ref[...] + route[:, 2:3] * _unpack_rows(ybuf[slot, 0].reshape(tm, w))
           + route[:, 3:4] * _unpack_rows(ybuf[slot, 1].reshape(tm, w)))
    if final:
        ms = jnp.mean(out * out, axis=-1, keepdims=True)
        out = out * lax.rsqrt(ms + RMS_EPS) * g_ref[...]
    o_ref[...] = out


def _combine(dest, r, route, y, tiles_per_seq, final_gain=None):
    n_tok, d = r.shape
    tm = MOE_TM
    final = final_gain is not None
    if final:
        real_tiles = tiles_per_seq - P0 // tm
        tile_of = lambda i: (i // real_tiles) * tiles_per_seq + P0 // tm + i % real_tiles
        n_out_tiles = (n_tok // tm // tiles_per_seq) * real_tiles
    else:
        tile_of = lambda i: i
        n_out_tiles = n_tok // tm
    in_specs = [
        pl.BlockSpec((tm, d), lambda i, *_: (tile_of(i), 0)),
        pl.BlockSpec((tm, LANES), lambda i, *_: (tile_of(i), 0)),
        pl.BlockSpec(memory_space=pl.ANY),
    ]
    args = [dest, r, route, y]
    if final:
        in_specs.append(pl.BlockSpec((1, d), lambda i, *_: (0, 0)))
        args.append(final_gain)
    return pl.pallas_call(
        functools.partial(_combine_kernel, tile_of=tile_of, final=final, n_tok=n_tok),
        grid_spec=pltpu.PrefetchScalarGridSpec(
            num_scalar_prefetch=1,
            grid=(n_out_tiles,),
            in_specs=in_specs,
            out_specs=pl.BlockSpec((tm, d), lambda i, *_: (i, 0)),
            scratch_shapes=[pltpu.VMEM((2, 2, tm) + y.shape[1:], y.dtype),
                            pltpu.SemaphoreType.DMA((2,))],
        ),
        out_shape=jax.ShapeDtypeStruct((n_out_tiles * tm, d), F32),
        compiler_params=pltpu.CompilerParams(
            dimension_semantics=("arbitrary",),
            vmem_limit_bytes=_vmem_limit(8 * tm * d * 4 + 4 * tm * d * 4 + (8 << 20))),
        name="moe_combine_final" if final else "moe_combine",
    )(*args)


def _routing_tables(route_t, counts, n_tiles_max, valid_tok):
    tm = MOE_TM
    cnt = counts[0, ROUTE_FIRST_EXPERT_LANE:ROUTE_FIRST_EXPERT_LANE + N_EXPERTS].astype(I32)
    ntile = (cnt + tm - 1) // tm
    tile_end = jnp.cumsum(ntile)
    tile_start = tile_end - ntile
    n_act = tile_end[-1:]
    e_ids = route_t[0:2].astype(I32)
    rank = route_t[4:6].astype(I32)
    row_start = tile_start * tm
    base = jnp.zeros_like(e_ids)
    for e in range(N_EXPERTS):
        base = jnp.where(e_ids == e, row_start[e], base)
    dest = jnp.where(valid_tok[None, :], base + rank, 0).reshape(-1)
    all_tiles = jnp.arange(n_tiles_max, dtype=I32)
    clamped = jnp.minimum(all_tiles, n_act[0] - 1)
    tile_e = jnp.sum((clamped[:, None] >= tile_end[None, :]).astype(I32), axis=1)
    tile_e = jnp.minimum(tile_e, N_EXPERTS - 1)
    first = jnp.concatenate([jnp.ones((1,), I32), (tile_e[1:] != tile_e[:-1]).astype(I32)])
    end_of_mine = tile_end[tile_e]
    nxt = jnp.where(end_of_mine < n_act[0], tile_e[jnp.minimum(end_of_mine, n_tiles_max - 1)], -1)
    is_expert_tail = jnp.any((all_tiles[:, None] == tile_end[None, :] - 1) & (ntile[None, :] > 0), axis=1)
    zflag = (is_expert_tail | (all_tiles >= n_act[0])).astype(I32)
    return dest, tile_e, first, nxt.astype(I32), n_act.astype(I32), zflag


def kernel(x, meta, norm_mix, w_in, b_forget, w_attn_out, conv_w, conv_b, conv_ln_g, conv_ln_b,
           w_conv_out, w_out, norm_ffn, w_router_group, b_router_group, w_router_expert,
           b_router_expert, w_gate_up, w_down, norm_final):
    batch, seq, d = x.shape
    depth = w_in.shape[0]
    heads = b_forget.shape[1]
    att_w = heads * HEAD_DIM
    conv_c = conv_b.shape[1]
    lp = P0 + seq
    tiles_per_seq = lp // SEQ_TILE
    n_tok = batch * lp
    assert seq % SEQ_TILE == 0 and n_tok % INPROJ_TM == 0
    assert (3 * att_w) % INPROJ_TN == 0 and (2 * conv_c + 2 * d) % INPROJ_TN == 0 and d % 2 == 0
    assert att_w % LANES == 0 and conv_c % LANES == 0 and SEQ_TILE == MOE_TM

    r = jnp.pad(x, ((0, 0), (P0, 0), (0, 0)))
    r = lax.dynamic_update_slice(
        r, jnp.broadcast_to(meta.astype(x.dtype)[None], (batch, N_META, d)), (0, META0, 0))
    r = r.reshape(n_tok, d)

    c_f = 3 * att_w
    c_u = c_f + heads
    m_q, m_k, m_v = 0, att_w, 2 * att_w
    m_ua = 3 * att_w
    m_ub = m_ua + conv_c
    m_ga = m_ub + conv_c
    m_gc = m_ga + d

    valid_tok = jnp.tile(jnp.arange(lp) >= META0, batch)
    n_pairs = 2 * batch * (lp - META0)
    n_tiles_max = n_pairs // MOE_TM + N_EXPERTS
    n_rows = n_tiles_max * MOE_TM

    out = None
    for l in range(depth):
        wf = jnp.pad(w_in[l, :, c_f:c_u], ((0, 0), (0, LANES - heads))).astype(BF16)
        z, f = _inproj(r, norm_mix[l][None], w_in[l, :, :c_f].astype(BF16),
                       w_in[l, :, c_u:].astype(BF16), wf)
        cum = _forget_cumsum(f.reshape(batch, lp, LANES),
                             jnp.pad(b_forget[l], (0, LANES - heads))[None])
        z3 = z.reshape(batch, lp, -1)
        att = _attention(z3, cum, m_q, m_k, m_v).reshape(n_tok, att_w)
        conv_w_pad = jnp.concatenate([conv_w[l], jnp.zeros((1, conv_c), F32)], axis=0)
        cv = _conv_branch(z3, m_ua, m_ub, conv_w_pad, conv_b[l][None], conv_ln_g[l][None],
                          conv_ln_b[l][None]).reshape(n_tok, conv_c)

        wr = jnp.zeros((d, LANES), F32)
        wr = wr.at[:, :N_GROUPS].set(w_router_group[l])
        wr = wr.at[:, N_GROUPS:N_GROUPS + N_EXPERTS].set(w_router_expert[l]).astype(BF16)
        br = jnp.zeros((1, LANES), F32)
        br = br.at[0, :N_GROUPS].set(b_router_group[l])
        br = br.at[0, N_GROUPS:N_GROUPS + N_EXPERTS].set(b_router_expert[l])
        r, n2, route, route_t, counts = _merge_route(
            att, cv, z, m_ga, m_gc, r, w_attn_out[l].astype(BF16), w_conv_out[l].astype(BF16),
            w_out[l].astype(BF16), norm_ffn[l][None], wr, br, tiles_per_seq)

        dest, tile_e, first, nxt, n_act, zflag = _routing_tables(route_t, counts, n_tiles_max, valid_tok)
        xs = _dispatch(dest, zflag, n2, n_rows, tiles_per_seq)
        y = _experts(tile_e, first, nxt, n_act, xs, w_gate_up, w_down, l)
        if l + 1 < depth:
            r = _combine(dest, r, route, y, tiles_per_seq)
        else:
            out = _combine(dest, r, route, y, tiles_per_seq, final_gain=norm_final[None])
    return out.reshape(batch, seq, d)
```

```python
import functools

import jax
import jax.numpy as jnp
from jax import lax
from jax.experimental import pallas as pl
from jax.experimental.pallas import tpu as pltpu

F32 = jnp.float32
BF16 = jnp.bfloat16
I32 = jnp.int32
U32 = jnp.uint32

LANES = 128
SUBLANES = 8
MXU_DIM = 256
VMEM_BYTES_V7X = 64 * 1024 * 1024

N_META = 16
HEAD_DIM = 64
HEADS_PER_BLOCK = LANES // HEAD_DIM
ATT_HEADS = 16
ATT_BLOCK = ATT_HEADS * HEAD_DIM
CONV_K = 31
N_GROUPS = 4
EXPERTS_PER_GROUP = 8
N_EXPERTS = N_GROUPS * EXPERTS_PER_GROUP
LOG2_E = 1.4426950408889634
RMS_EPS = 1e-6
LN_EPS = 1e-5

SEQ_TILE = 256
P0 = SEQ_TILE
META0 = P0 - N_META
INPROJ_TM = 1024
INPROJ_TN = 1024
MOE_TM = 256
CONV_HALO = 32
DMA_UNROLL = 8
ROUTE_FIRST_EXPERT_LANE = N_GROUPS
ROUTE_FIELDS = 8
MASK_VALUE = -1e30


def _vmem_limit(nbytes):
    return int(min(max(nbytes, 16 * 1024 * 1024), VMEM_BYTES_V7X - 6 * 1024 * 1024))


def _sigmoid(x):
    return 1.0 / (1.0 + jnp.exp(-x))


def _pack_rows(v):
    c = v.shape[1] // 2
    lo = lax.bitcast_convert_type(v[:, :c].astype(BF16).astype(F32), U32)
    hi = lax.bitcast_convert_type(v[:, c:].astype(BF16).astype(F32), U32)
    return (hi & jnp.uint32(0xFFFF0000)) | (lo >> 16)


def _unpack_rows(w):
    lo = lax.bitcast_convert_type(w << 16, F32)
    hi = lax.bitcast_convert_type(w & jnp.uint32(0xFFFF0000), F32)
    return jnp.concatenate([lo, hi], axis=1)


def _inproj_kernel(r_ref, g_ref, wa_ref, wb_ref, wf_ref, z_ref, f_ref, n_sc, *, na):
    j = pl.program_id(1)

    @pl.when(j == 0)
    def _():
        x = r_ref[...]
        ms = jnp.mean(x * x, axis=-1, keepdims=True)
        n = (x * lax.rsqrt(ms + RMS_EPS) * g_ref[...]).astype(BF16)
        n_sc[...] = n
        f_ref[...] = jnp.dot(n, wf_ref[...], preferred_element_type=F32)

    @pl.when(j < na)
    def _():
        z_ref[...] = jnp.dot(n_sc[...], wa_ref[...], preferred_element_type=F32).astype(z_ref.dtype)

    @pl.when(j >= na)
    def _():
        z_ref[...] = jnp.dot(n_sc[...], wb_ref[...], preferred_element_type=F32).astype(z_ref.dtype)


def _inproj(r, gain, w_a, w_b, wf):
    n_tok, d = r.shape
    tm, tn = INPROJ_TM, INPROJ_TN
    na, nb = w_a.shape[1] // tn, w_b.shape[1] // tn
    cols = (na + nb) * tn
    vmem = 2 * tm * d * 4 + 4 * d * tn * 2 + 2 * tm * tn * 2 + tm * d * 2 + 2 * (d * 2 + tm * 4) * LANES
    return pl.pallas_call(
        functools.partial(_inproj_kernel, na=na),
        grid=(n_tok // tm, na + nb),
        in_specs=[
            pl.BlockSpec((tm, d), lambda i, j: (i, 0)),
            pl.BlockSpec((1, d), lambda i, j: (0, 0)),
            pl.BlockSpec((d, tn), lambda i, j: (0, jnp.minimum(j, na - 1))),
            pl.BlockSpec((d, tn), lambda i, j: (0, jnp.maximum(j - na, 0))),
            pl.BlockSpec((d, LANES), lambda i, j: (0, 0)),
        ],
        out_specs=[
            pl.BlockSpec((tm, tn), lambda i, j: (i, j)),
            pl.BlockSpec((tm, LANES), lambda i, j: (i, 0)),
        ],
        out_shape=[
            jax.ShapeDtypeStruct((n_tok, cols), BF16),
            jax.ShapeDtypeStruct((n_tok, LANES), F32),
        ],
        scratch_shapes=[pltpu.VMEM((tm, d), BF16)],
        compiler_params=pltpu.CompilerParams(
            dimension_semantics=("parallel", "arbitrary"),
            vmem_limit_bytes=_vmem_limit(vmem + (8 << 20))),
        name="inproj",
    )(r, gain, w_a, w_b, wf)


def _cum_kernel(f_ref, b_ref, o_ref):
    ch = MXU_DIM
    lp = f_ref.shape[1]
    row = lax.broadcasted_iota(I32, (ch, ch), 0)
    col = lax.broadcasted_iota(I32, (ch, ch), 1)
    lower = (col <= row).astype(F32)
    carry = jnp.zeros((1, LANES), F32)
    for c in range(lp // ch):
        x = f_ref[0, c * ch:(c + 1) * ch, :] + b_ref[...]
        log_f = jnp.minimum(x, 0.0) - jnp.log1p(jnp.exp(-jnp.abs(x)))
        loc = jnp.dot(lower, log_f, preferred_element_type=F32,
                      precision=lax.Precision.HIGHEST) + carry
        o_ref[0, c * ch:(c + 1) * ch, :] = loc
        carry = loc[ch - 1:ch, :]


def _forget_cumsum(f3, b_forget):
    batch, lp, _ = f3.shape
    return pl.pallas_call(
        _cum_kernel,
        grid=(batch,),
        in_specs=[
            pl.BlockSpec((1, lp, LANES), lambda b: (b, 0, 0)),
            pl.BlockSpec((1, LANES), lambda b: (0, 0)),
        ],
        out_specs=pl.BlockSpec((1, lp, LANES), lambda b: (b, 0, 0)),
        out_shape=jax.ShapeDtypeStruct((batch, lp, LANES), F32),
        compiler_params=pltpu.CompilerParams(dimension_semantics=("parallel",)),
        name="forget_cumsum",
    )(f3, b_forget)


def _attn_kernel(q_ref, k_ref, v_ref, c_ref, o_ref, kaug_sc, vt_sc, qaug_sc, s_a, s_b,
                 m_sc, l_sc, acc_sc):
    tq = q_ref.shape[1]
    tk = tq
    lp = k_ref.shape[1]
    hg = pl.program_id(1)
    qi = pl.program_id(2)
    lane = lax.broadcasted_iota(I32, (1, LANES), 1)
    nt_dims = (((1,), (1,)), ((), ()))

    def own_lanes(h):
        return lane < HEAD_DIM if h % HEADS_PER_BLOCK == 0 else lane >= HEAD_DIM

    def bias_lane(h):
        return HEAD_DIM if h % HEADS_PER_BLOCK == 0 else 0

    def block_lanes(h):
        blk = h // HEADS_PER_BLOCK
        return slice(blk * LANES, (blk + 1) * LANES)

    @pl.when(qi == 0)
    def _():
        for c in range(lp // tk):
            rows = slice(c * tk, (c + 1) * tk)
            vt_sc[:, rows] = v_ref[0, rows, :].astype(F32).T.astype(BF16)
            cum = c_ref[0, rows, :]
            key_pos = c * tk + lax.broadcasted_iota(I32, (tk, 1), 0)
            for h in range(ATT_HEADS):
                head = hg * ATT_HEADS + h
                col = jnp.sum(jnp.where(lane == head, cum, 0.0), axis=-1, keepdims=True) * LOG2_E
                hi = col.astype(BF16).astype(F32)
                rem = col - hi
                mid = rem.astype(BF16).astype(F32)
                low = rem - mid
                hi = jnp.where(key_pos < META0, -MASK_VALUE, hi)
                a = bias_lane(h)
                bias = jnp.where(lane == a, hi, jnp.where(lane == a + 1, mid,
                                                          jnp.where(lane == a + 2, low, 0.0)))
                kaug_sc[h, rows, :] = jnp.where(own_lanes(h), k_ref[0, rows, block_lanes(h)],
                                                bias.astype(BF16))

    for h in range(ATT_HEADS):
        a = bias_lane(h)
        minus_one = jnp.where((lane >= a) & (lane < a + 3), -1.0, 0.0)
        scaled = q_ref[0, :, block_lanes(h)].astype(F32) * (HEAD_DIM ** -0.5 * LOG2_E)
        qaug_sc[h] = jnp.where(own_lanes(h), scaled, minus_one).astype(BF16)
        m_sc[h] = jnp.full((1, tq), MASK_VALUE, F32)
        l_sc[h] = jnp.zeros((1, tq), F32)
        acc_sc[h] = jnp.zeros((HEAD_DIM, tq), F32)

    def scores_into(dst, j):
        s0 = pl.multiple_of(j * tk, tk)
        for h in range(ATT_HEADS):
            dst[h] = lax.dot_general(kaug_sc[h, pl.ds(s0, tk), :], qaug_sc[h], nt_dims,
                                     preferred_element_type=F32)

    def softmax_pv(src, j, diagonal):
        s0 = pl.multiple_of(j * tk, tk)
        for h in range(ATT_HEADS):
            st = src[h]
            if diagonal:
                key = lax.broadcasted_iota(I32, (tk, tq), 0)
                qry = lax.broadcasted_iota(I32, (tk, tq), 1)
                st = jnp.where(key <= qry, st, MASK_VALUE)
            m = m_sc[h]
            m_new = jnp.maximum(m, jnp.max(st, axis=0, keepdims=True))
            alpha = jnp.exp2(m - m_new)
            p = jnp.exp2(st - m_new)
            l_sc[h] = alpha * l_sc[h] + jnp.sum(p, axis=0, keepdims=True)
            pv = jnp.dot(vt_sc[h * HEAD_DIM:(h + 1) * HEAD_DIM, pl.ds(s0, tk)], p.astype(BF16),
                         preferred_element_type=F32)
            acc_sc[h] = alpha * acc_sc[h] + pv
            m_sc[h] = m_new

    scores_into(s_a, 0)

    def pair(i, _):
        j = 2 * i
        scores_into(s_b, j + 1)
        softmax_pv(s_a, j, False)
        scores_into(s_a, j + 2)
        softmax_pv(s_b, j + 1, False)
        return 0

    lax.fori_loop(0, qi // 2, pair, 0)

    @pl.when(qi % 2 == 0)
    def _():
        softmax_pv(s_a, qi, True)

    @pl.when(qi % 2 == 1)
    def _():
        scores_into(s_b, qi)
        softmax_pv(s_a, qi - 1, False)
        softmax_pv(s_b, qi, True)

    att_t = jnp.concatenate([acc_sc[h] / l_sc[h] for h in range(ATT_HEADS)], axis=0)
    o_ref[0] = att_t.T.astype(o_ref.dtype)


def _attention(z3, cum, col_q, col_k, col_v):
    batch, lp, _ = z3.shape
    att_w = col_k - col_q
    tq = SEQ_TILE
    qb, kb, vb = col_q // ATT_BLOCK, col_k // ATT_BLOCK, col_v // ATT_BLOCK
    vmem = (2 * (2 * lp + 2 * tq) * ATT_BLOCK * 2 + 2 * lp * LANES * 4
            + (ATT_HEADS * lp * LANES + ATT_BLOCK * lp + ATT_HEADS * tq * LANES) * 2
            + (2 * ATT_HEADS * tq * tq + ATT_HEADS * (HEAD_DIM + 2 * SUBLANES) * tq) * 4
            + (8 << 20))
    return pl.pallas_call(
        _attn_kernel,
        grid=(batch, att_w // ATT_BLOCK, lp // tq),
        in_specs=[
            pl.BlockSpec((1, tq, ATT_BLOCK), lambda b, h, i: (b, i, qb + h)),
            pl.BlockSpec((1, lp, ATT_BLOCK), lambda b, h, i: (b, 0, kb + h)),
            pl.BlockSpec((1, lp, ATT_BLOCK), lambda b, h, i: (b, 0, vb + h)),
            pl.BlockSpec((1, lp, LANES), lambda b, h, i: (b, 0, 0)),
        ],
        out_specs=pl.BlockSpec((1, tq, ATT_BLOCK), lambda b, h, i: (b, i, h)),
        out_shape=jax.ShapeDtypeStruct((batch, lp, att_w), BF16),
        scratch_shapes=[
            pltpu.VMEM((ATT_HEADS, lp, LANES), BF16), pltpu.VMEM((ATT_BLOCK, lp), BF16),
            pltpu.VMEM((ATT_HEADS, tq, LANES), BF16),
            pltpu.VMEM((ATT_HEADS, tq, tq), F32), pltpu.VMEM((ATT_HEADS, tq, tq), F32),
            pltpu.VMEM((ATT_HEADS, 1, tq), F32), pltpu.VMEM((ATT_HEADS, 1, tq), F32),
            pltpu.VMEM((ATT_HEADS, HEAD_DIM, tq), F32),
        ],
        compiler_params=pltpu.CompilerParams(
            dimension_semantics=("parallel", "parallel", "arbitrary"),
            vmem_limit_bytes=_vmem_limit(vmem)),
        name="fox_attention",
    )(z3, z3, z3, cum)


def _conv_kernel(a_ref, b_ref, w_ref, cb_ref, g_ref, lb_ref, o_ref, zbuf, ybuf, zshift):
    tt = a_ref.shape[1]
    ch = a_ref.shape[2]
    t = pl.program_id(1)

    @pl.when(t == 0)
    def _():
        zbuf[0:CONV_HALO, :] = jnp.zeros((CONV_HALO, ch), F32)

    z = a_ref[0].astype(F32) * _sigmoid(b_ref[0].astype(F32))
    rows = t * tt + lax.broadcasted_iota(I32, (tt, 1), 0)
    zbuf[CONV_HALO:CONV_HALO + tt, :] = jnp.where(rows >= META0, z, 0.0)

    first = CONV_HALO - (CONV_K - 1)

    def chan_block(cb, _):
        c0 = pl.multiple_of(cb * LANES, LANES)
        for res in range(SUBLANES):
            span = tt + ((CONV_K - 1 - res) // SUBLANES) * SUBLANES
            zshift[res, 0:span, :] = zbuf[first + res:first + res + span, pl.ds(c0, LANES)]
        acc = jnp.zeros((tt, LANES), F32) + cb_ref[:, pl.ds(c0, LANES)]
        for k in range(CONV_K):
            base = (k // SUBLANES) * SUBLANES
            acc = acc + w_ref[k:k + 1, pl.ds(c0, LANES)] * zshift[k % SUBLANES, base:base + tt, :]
        ybuf[:, pl.ds(c0, LANES)] = acc
        return 0

    lax.fori_loop(0, ch // LANES, chan_block, 0)

    y = ybuf[...]
    mu = jnp.mean(y, axis=-1, keepdims=True)
    yc = y - mu
    var = jnp.mean(yc * yc, axis=-1, keepdims=True)
    zn = yc * lax.rsqrt(var + LN_EPS) * g_ref[...] + lb_ref[...]
    o_ref[0] = (zn * _sigmoid(zn)).astype(o_ref.dtype)
    zbuf[0:CONV_HALO, :] = zbuf[tt:tt + CONV_HALO, :]


def _conv_branch(z3, col_a, col_b, conv_w, conv_b, ln_g, ln_b):
    batch, lp, _ = z3.shape
    ch = conv_b.shape[1]
    tt = SEQ_TILE
    ab, bb = col_a // ch, col_b // ch
    return pl.pallas_call(
        _conv_kernel,
        grid=(batch, lp // tt),
        in_specs=[
            pl.BlockSpec((1, tt, ch), lambda b, t: (b, t, ab)),
            pl.BlockSpec((1, tt, ch), lambda b, t: (b, t, bb)),
            pl.BlockSpec(conv_w.shape, lambda b, t: (0, 0)),
            pl.BlockSpec((1, ch), lambda b, t: (0, 0)),
            pl.BlockSpec((1, ch), lambda b, t: (0, 0)),
            pl.BlockSpec((1, ch), lambda b, t: (0, 0)),
        ],
        out_specs=pl.BlockSpec((1, tt, ch), lambda b, t: (b, t, 0)),
        out_shape=jax.ShapeDtypeStruct((batch, lp, ch), BF16),
        scratch_shapes=[
            pltpu.VMEM((tt + CONV_HALO, ch), F32), pltpu.VMEM((tt, ch), F32),
            pltpu.VMEM((SUBLANES, tt + ((CONV_K - 1) // SUBLANES) * SUBLANES, LANES), F32),
        ],
        compiler_params=pltpu.CompilerParams(dimension_semantics=("parallel", "arbitrary")),
        name="conv_branch",
    )(z3, z3, conv_w, conv_b, ln_g, ln_b)


def _merge_route_kernel(att_ref, cv_ref, ga0_ref, ga1_ref, gc0_ref, gc1_ref, r_ref, wao_ref, wco_ref,
                        wo_ref, g_ref, wr_ref, br_ref, ro_ref, n_ref, route_ref, route_t_ref, cnt_ref,
                        cnt_sc, rn_sc, *, tiles_per_seq):
    i = pl.program_id(0)
    tm = r_ref.shape[0]

    @pl.when(i == 0)
    def _():
        cnt_sc[...] = jnp.zeros_like(cnt_sc)
        rn_sc[...] = jnp.zeros_like(rn_sc)

    br_att = jnp.dot(att_ref[...], wao_ref[...], preferred_element_type=F32)

    r_prev = rn_sc[...]
    ms = jnp.mean(r_prev * r_prev, axis=-1, keepdims=True)
    n = r_prev * lax.rsqrt(ms + RMS_EPS) * g_ref[...]
    n_ref[...] = _pack_rows(n).reshape(n_ref.shape)
    logits = jnp.dot(n.astype(BF16), wr_ref[...], preferred_element_type=F32) + br_ref[...]

    br_conv = jnp.dot(cv_ref[...], wco_ref[...], preferred_element_type=F32)
    lane = lax.broadcasted_iota(I32, logits.shape, 1)
    big = jnp.int32(4 * LANES)
    first_e = ROUTE_FIRST_EXPERT_LANE

    gl = jnp.where(lane < N_GROUPS, logits, -jnp.inf)
    gmax = jnp.max(gl, axis=-1, keepdims=True)
    gsum = jnp.sum(jnp.exp(gl - gmax), axis=-1, keepdims=True)
    g_w = 1.0 / gsum
    g_idx = jnp.min(jnp.where(gl == gmax, lane, big), axis=-1, keepdims=True)

    in_group = (lane >= first_e + g_idx * EXPERTS_PER_GROUP) & \
               (lane < first_e + (g_idx + 1) * EXPERTS_PER_GROUP)
    el = jnp.where(in_group, logits, -jnp.inf)
    v0 = jnp.max(el, axis=-1, keepdims=True)
    i0 = jnp.min(jnp.where(el == v0, lane, big), axis=-1, keepdims=True)
    el = jnp.where(lane == i0, -jnp.inf, el)
    v1 = jnp.max(el, axis=-1, keepdims=True)
    i1 = jnp.min(jnp.where(el == v1, lane, big), axis=-1, keepdims=True)
    e1 = jnp.exp(v1 - v0)
    w0 = g_w / (1.0 + e1)
    w1 = g_w * e1 / (1.0 + e1)

    pos = ((i + tiles_per_seq - 1) % tiles_per_seq) * tm + lax.broadcasted_iota(I32, (tm, 1), 0)
    valid = (pos >= META0) & (i >= 1)
    onehot = jnp.where(((lane == i0) | (lane == i1)) & valid, 1.0, 0.0)
    rr = lax.broadcasted_iota(I32, (tm, tm), 0)
    cc = lax.broadcasted_iota(I32, (tm, tm), 1)
    lower = jnp.where(cc < rr, 1.0, 0.0).astype(BF16)
    rank = jnp.dot(lower, onehot.astype(BF16), preferred_element_type=F32) + cnt_sc[0:1, :]

    g_att = jnp.concatenate([ga0_ref[...], ga1_ref[...]], axis=1).astype(F32)
    g_conv = jnp.concatenate([gc0_ref[...], gc1_ref[...]], axis=1).astype(F32)
    merged = _sigmoid(g_att) * br_att + _sigmoid(g_conv) * br_conv
    r_new = r_ref[...] + jnp.dot(merged.astype(BF16), wo_ref[...], preferred_element_type=F32)
    ro_ref[...] = r_new
    rn_sc[...] = r_new

    rank0 = jnp.sum(jnp.where(lane == i0, rank, 0.0), axis=-1, keepdims=True)
    rank1 = jnp.sum(jnp.where(lane == i1, rank, 0.0), axis=-1, keepdims=True)
    cnt_sc[0:1, :] = cnt_sc[0:1, :] + jnp.sum(onehot, axis=0, keepdims=True)
    cnt_ref[...] = jnp.broadcast_to(cnt_sc[0:1, :], cnt_ref.shape)

    w0 = jnp.where(valid, w0, 0.0)
    w1 = jnp.where(valid, w1, 0.0)
    out = jnp.where(lane == 0, (i0 - first_e).astype(F32), 0.0)
    out = jnp.where(lane == 1, (i1 - first_e).astype(F32), out)
    out = jnp.where(lane == 2, w0, out)
    out = jnp.where(lane == 3, w1, out)
    out = jnp.where(lane == 4, rank0, out)
    out = jnp.where(lane == 5, rank1, out)
    route_ref[...] = out
    route_t_ref[...] = out.T[:ROUTE_FIELDS, :]


def _merge_route(att, cv, z, col_ga, col_gc, r, wao, wco, wo, gain, wr, br, tiles_per_seq):
    n_tok, d = r.shape
    aw = att.shape[1]
    cw = cv.shape[1]
    tm = SEQ_TILE
    half = d // 2
    gab, gcb = col_ga // half, col_gc // half
    const = lambda i: (0, 0)
    single = pl.Buffered(1)
    vmem = (2 * tm * (aw * 2 + cw * 2 + 2 * d * 2 + 3 * d * 4 + LANES * 4)
            + (aw * d + cw * d + d * d + d * LANES) * 2 + 8 * tm * d * 4)
    n_tiles = n_tok // tm
    cur = lambda i: jnp.minimum(i, n_tiles - 1)
    prev = lambda i: jnp.maximum(i - 1, 0)
    return pl.pallas_call(
        functools.partial(_merge_route_kernel, tiles_per_seq=tiles_per_seq),
        grid=(n_tiles + 1,),
        in_specs=[
            pl.BlockSpec((tm, aw), lambda i: (cur(i), 0)),
            pl.BlockSpec((tm, cw), lambda i: (cur(i), 0)),
            pl.BlockSpec((tm, half), lambda i: (cur(i), gab)),
            pl.BlockSpec((tm, half), lambda i: (cur(i), gab + 1)),
            pl.BlockSpec((tm, half), lambda i: (cur(i), gcb)),
            pl.BlockSpec((tm, half), lambda i: (cur(i), gcb + 1)),
            pl.BlockSpec((tm, d), lambda i: (cur(i), 0)),
            pl.BlockSpec(wao.shape, const, pipeline_mode=single),
            pl.BlockSpec(wco.shape, const, pipeline_mode=single),
            pl.BlockSpec(wo.shape, const, pipeline_mode=single),
            pl.BlockSpec((1, d), const),
            pl.BlockSpec(wr.shape, const, pipeline_mode=single),
            pl.BlockSpec((1, LANES), const),
        ],
        out_specs=[
            pl.BlockSpec((tm, d), lambda i: (cur(i), 0)),
            pl.BlockSpec((tm, d // 2 // LANES, LANES), lambda i: (prev(i), 0, 0)),
            pl.BlockSpec((tm, LANES), lambda i: (prev(i), 0)),
            pl.BlockSpec((ROUTE_FIELDS, tm), lambda i: (0, prev(i))),
            pl.BlockSpec((8, LANES), const),
        ],
        out_shape=[
            jax.ShapeDtypeStruct((n_tok, d), F32),
            jax.ShapeDtypeStruct((n_tok, d // 2 // LANES, LANES), U32),
            jax.ShapeDtypeStruct((n_tok, LANES), F32),
            jax.ShapeDtypeStruct((ROUTE_FIELDS, n_tok), F32),
            jax.ShapeDtypeStruct((8, LANES), F32),
        ],
        scratch_shapes=[pltpu.VMEM((8, LANES), F32), pltpu.VMEM((tm, d), F32)],
        compiler_params=pltpu.CompilerParams(
            dimension_semantics=("arbitrary",), vmem_limit_bytes=_vmem_limit(vmem + tm * d * 4)),
        name="merge_route",
    )(att, cv, z, z, z, z, r, wao, wco, wo, gain, wr, br)


def _dispatch_kernel(dest_ref, zflag_ref, n_ref, xs_ref, zero_sc, stage, sem, zsem, *, tiles_per_seq):
    i = pl.program_id(0)
    tm = zero_sc.shape[0]
    n_tiles = xs_ref.shape[0] // tm
    n_tok = pl.num_programs(0) * tm

    def zero_copy(t):
        return pltpu.make_async_copy(zero_sc, xs_ref.at[pl.ds(pl.multiple_of(t * tm, tm), tm)], zsem)

    @pl.when(i == 0)
    def _():
        zero_sc[...] = jnp.zeros_like(zero_sc)

        def issue_zero(t, _):
            @pl.when(zflag_ref[t] != 0)
            def _():
                zero_copy(t).start()
            return 0

        def drain_zero(t, _):
            @pl.when(zflag_ref[t] != 0)
            def _():
                zero_copy(t).wait()
            return 0

        lax.fori_loop(0, n_tiles, issue_zero, 0)
        lax.fori_loop(0, n_tiles, drain_zero, 0)

    slot = i % 2
    last = pl.num_programs(0) - 1

    def first_row_group(step):
        return jnp.where(step % tiles_per_seq == 0, META0, 0) // SUBLANES

    def row_copy(step, buf, g, u, k):
        t = g * SUBLANES + u
        d = dest_ref[k * n_tok + step * tm + t]
        return pltpu.make_async_copy(stage.at[buf, pl.ds(t, 1)], xs_ref.at[pl.ds(d, 1)], sem.at[buf])

    def drain(step, buf):
        one_row = pltpu.make_async_copy(stage.at[buf, pl.ds(0, 1)], xs_ref.at[pl.ds(0, 1)], sem.at[buf])

        def body(g, _):
            for _u in range(2 * SUBLANES):
                one_row.wait()
            return 0
        lax.fori_loop(first_row_group(step), tm // SUBLANES, body, 0)

    @pl.when(i >= 2)
    def _():
        drain(i - 2, slot)

    stage[slot] = n_ref[...]

    def issue(g, _):
        for u in range(SUBLANES):
            row_copy(i, slot, g, u, 0).start(priority=0)
            row_copy(i, slot, g, u, 1).start(priority=1)
        return 0

    lax.fori_loop(first_row_group(i), tm // SUBLANES, issue, 0)

    @pl.when(i == last)
    def _():
        @pl.when(i >= 1)
        def _():
            drain(i - 1, 1 - slot)
        drain(i, slot)


def _dispatch(dest, zflag, n, n_rows, tiles_per_seq):
    n_tok = n.shape[0]
    row = n.shape[1:]
    tm = MOE_TM
    return pl.pallas_call(
        functools.partial(_dispatch_kernel, tiles_per_seq=tiles_per_seq),
        grid_spec=pltpu.PrefetchScalarGridSpec(
            num_scalar_prefetch=2,
            grid=(n_tok // tm,),
            in_specs=[pl.BlockSpec((tm,) + row, lambda i, *_: (i, 0, 0))],
            out_specs=pl.BlockSpec(memory_space=pl.ANY),
            scratch_shapes=[pltpu.VMEM((tm,) + row, n.dtype), pltpu.VMEM((2, tm) + row, n.dtype),
                            pltpu.SemaphoreType.DMA((2,)), pltpu.SemaphoreType.DMA],
        ),
        out_shape=jax.ShapeDtypeStruct((n_rows,) + row, n.dtype),
        compiler_params=pltpu.CompilerParams(dimension_semantics=("arbitrary",),
                                             has_side_effects=True),
        name="moe_dispatch",
    )(dest, zflag, n)


def _expert_kernel(te_ref, first_ref, nxt_ref, na_ref, x_ref, wgu_hbm, wdn_hbm, y_ref,
                   gu_stage, dn_stage, gu_bf, dn_bf, sem, *, layer):
    i = pl.program_id(0)
    active = i < na_ref[0]

    def weight_copies(e):
        return (pltpu.make_async_copy(wgu_hbm.at[layer, e], gu_stage, sem.at[0]),
                pltpu.make_async_copy(wdn_hbm.at[layer, e], dn_stage, sem.at[1]))

    @pl.when(i == 0)
    def _():
        for cp in weight_copies(te_ref[0]):
            cp.start()

    @pl.when(active & (first_ref[i] != 0))
    def _():
        for cp in weight_copies(te_ref[i]):
            cp.wait()
        gu_bf[...] = gu_stage[...].astype(BF16)
        dn_bf[...] = dn_stage[...].astype(BF16)

        @pl.when(nxt_ref[i] >= 0)
        def _():
            for cp in weight_copies(nxt_ref[i]):
                cp.start()

    @pl.when(active)
    def _():
        de = dn_bf.shape[0]
        tm = x_ref.shape[0]
        x = _unpack_rows(x_ref[...].reshape(tm, dn_bf.shape[1] // 2)).astype(BF16)
        h = jnp.dot(x, gu_bf[...], preferred_element_type=F32)
        a = h[:, :de]
        b = h[:, de:]
        act = a * _sigmoid(a) * b
        y = _pack_rows(jnp.dot(act.astype(BF16), dn_bf[...], preferred_element_type=F32))
        y_ref[...] = y.reshape(y_ref.shape)

    @pl.when(jnp.logical_not(active))
    def _():
        y_ref[...] = jnp.zeros_like(y_ref)


def _experts(tile_e, first, nxt, n_act, xs, w_gate_up, w_down, layer):
    n_rows = xs.shape[0]
    row = xs.shape[1:]
    w = row[0] * row[1]
    d = w_gate_up.shape[2]
    assert d == 2 * w and row[1] == LANES
    tm = MOE_TM
    de2 = w_gate_up.shape[3]
    de = w_down.shape[2]
    row_map = lambda i, te, fi, nx, na: (jnp.minimum(i, na[0] - 1), 0, 0)
    vmem = (d * de2 + de * d) * (4 + 2) + 6 * tm * d * 4 + 6 * tm * de2 * 4
    return pl.pallas_call(
        functools.partial(_expert_kernel, layer=layer),
        grid_spec=pltpu.PrefetchScalarGridSpec(
            num_scalar_prefetch=4,
            grid=(n_rows // tm,),
            in_specs=[
                pl.BlockSpec((tm,) + row, row_map),
                pl.BlockSpec(memory_space=pl.ANY),
                pl.BlockSpec(memory_space=pl.ANY),
            ],
            out_specs=pl.BlockSpec((tm,) + row, lambda i, *_: (i, 0, 0)),
            scratch_shapes=[
                pltpu.VMEM((d, de2), F32), pltpu.VMEM((de, d), F32),
                pltpu.VMEM((d, de2), BF16), pltpu.VMEM((de, d), BF16),
                pltpu.SemaphoreType.DMA((2,)),
            ],
        ),
        out_shape=jax.ShapeDtypeStruct(xs.shape, xs.dtype),
        compiler_params=pltpu.CompilerParams(
            dimension_semantics=("arbitrary",), vmem_limit_bytes=_vmem_limit(vmem)),
        name="moe_experts",
    )(tile_e, first, nxt, n_act, xs, w_gate_up, w_down)


def _combine_kernel(dest_ref, r_ref, route_ref, y_ref, *rest, tile_of, final, n_tok):
    if final:
        g_ref, o_ref, ybuf, sem = rest
    else:
        o_ref, ybuf, sem = rest
    tm = r_ref.shape[0]
    i = pl.program_id(0)
    slot = i % 2

    def issue(step, buf):
        tile = tile_of(step)

        def body(g, _):
            for u in range(SUBLANES):
                t = g * SUBLANES + u
                for k in range(2):
                    d = dest_ref[k * n_tok + tile * tm + t]
                    pltpu.make_async_copy(y_ref.at[pl.ds(d, 1)], ybuf.at[buf, k, pl.ds(t, 1)],
                                          sem.at[buf]).start(priority=k)
            return 0
        lax.fori_loop(0, tm // SUBLANES, body, 0)

    @pl.when(i == 0)
    def _():
        issue(i, slot)

    @pl.when(i + 1 < pl.num_programs(0))
    def _():
        issue(i + 1, 1 - slot)

    one_row = pltpu.make_async_copy(y_ref.at[pl.ds(0, 1)], ybuf.at[slot, 0, pl.ds(0, 1)], sem.at[slot])

    def drain(g, _):
        for _u in range(2 * SUBLANES):
            one_row.wait()
        return 0
    lax.fori_loop(0, tm // SUBLANES, drain, 0)

    route = route_ref[...]
    w = ybuf.shape[-2] * ybuf.shape[-1]
    out = (r_ref[...] + route[:, 2:3] * _unpack_rows(ybuf[slot, 0].reshape(tm, w))
           + route[:, 3:4] * _unpack_rows(ybuf[slot, 1].reshape(tm, w)))
    if final:
        ms = jnp.mean(out * out, axis=-1, keepdims=True)
        out = out * lax.rsqrt(ms + RMS_EPS) * g_ref[...]
    o_ref[...] = out


def _combine(dest, r, route, y, tiles_per_seq, final_gain=None):
    n_tok, d = r.shape
    tm = MOE_TM
    final = final_gain is not None
    if final:
        real_tiles = tiles_per_seq - P0 // tm
        tile_of = lambda i: (i // real_tiles) * tiles_per_seq + P0 // tm + i % real_tiles
        n_out_tiles = (n_tok // tm // tiles_per_seq) * real_tiles
    else:
        tile_of = lambda i: i
        n_out_tiles = n_tok // tm
    in_specs = [
        pl.BlockSpec((tm, d), lambda i, *_: (tile_of(i), 0)),
        pl.BlockSpec((tm, LANES), lambda i, *_: (tile_of(i), 0)),
        pl.BlockSpec(memory_space=pl.ANY),
    ]
    args = [dest, r, route, y]
    if final:
        in_specs.append(pl.BlockSpec((1, d), lambda i, *_: (0, 0)))
        args.append(final_gain)
    return pl.pallas_call(
        functools.partial(_combine_kernel, tile_of=tile_of, final=final, n_tok=n_tok),
        grid_spec=pltpu.PrefetchScalarGridSpec(
            num_scalar_prefetch=1,
            grid=(n_out_tiles,),
            in_specs=in_specs,
            out_specs=pl.BlockSpec((tm, d), lambda i, *_: (i, 0)),
            scratch_shapes=[pltpu.VMEM((2, 2, tm) + y.shape[1:], y.dtype),
                            pltpu.SemaphoreType.DMA((2,))],
        ),
        out_shape=jax.ShapeDtypeStruct((n_out_tiles * tm, d), F32),
        compiler_params=pltpu.CompilerParams(
            dimension_semantics=("arbitrary",),
            vmem_limit_bytes=_vmem_limit(8 * tm * d * 4 + 4 * tm * d * 4 + (8 << 20))),
        name="moe_combine_final" if final else "moe_combine",
    )(*args)


def _routing_tables(route_t, counts, n_tiles_max, valid_tok):
    tm = MOE_TM
    cnt = counts[0, ROUTE_FIRST_EXPERT_LANE:ROUTE_FIRST_EXPERT_LANE + N_EXPERTS].astype(I32)
    ntile = (cnt + tm - 1) // tm
    tile_end = jnp.cumsum(ntile)
    tile_start = tile_end - ntile
    n_act = tile_end[-1:]
    e_ids = route_t[0:2].astype(I32)
    rank = route_t[4:6].astype(I32)
    row_start = tile_start * tm
    base = jnp.zeros_like(e_ids)
    for e in range(N_EXPERTS):
        base = jnp.where(e_ids == e, row_start[e], base)
    dest = jnp.where(valid_tok[None, :], base + rank, 0).reshape(-1)
    all_tiles = jnp.arange(n_tiles_max, dtype=I32)
    clamped = jnp.minimum(all_tiles, n_act[0] - 1)
    tile_e = jnp.sum((clamped[:, None] >= tile_end[None, :]).astype(I32), axis=1)
    tile_e = jnp.minimum(tile_e, N_EXPERTS - 1)
    first = jnp.concatenate([jnp.ones((1,), I32), (tile_e[1:] != tile_e[:-1]).astype(I32)])
    end_of_mine = tile_end[tile_e]
    nxt = jnp.where(end_of_mine < n_act[0], tile_e[jnp.minimum(end_of_mine, n_tiles_max - 1)], -1)
    is_expert_tail = jnp.any((all_tiles[:, None] == tile_end[None, :] - 1) & (ntile[None, :] > 0), axis=1)
    zflag = (is_expert_tail | (all_tiles >= n_act[0])).astype(I32)
    return dest, tile_e, first, nxt.astype(I32), n_act.astype(I32), zflag


def kernel(x, meta, norm_mix, w_in, b_forget, w_attn_out, conv_w, conv_b, conv_ln_g, conv_ln_b,
           w_conv_out, w_out, norm_ffn, w_router_group, b_router_group, w_router_expert,
           b_router_expert, w_gate_up, w_down, norm_final):
    batch, seq, d = x.shape
    depth = w_in.shape[0]
    heads = b_forget.shape[1]
    att_w = heads * HEAD_DIM
    conv_c = conv_b.shape[1]
    lp = P0 + seq
    tiles_per_seq = lp // SEQ_TILE
    n_tok = batch * lp
    assert seq % SEQ_TILE == 0 and n_tok % INPROJ_TM == 0
    assert (3 * att_w) % INPROJ_TN == 0 and (2 * conv_c + 2 * d) % INPROJ_TN == 0 and d % 2 == 0
    assert att_w % LANES == 0 and conv_c % LANES == 0 and SEQ_TILE == MOE_TM

    r = jnp.pad(x, ((0, 0), (P0, 0), (0, 0)))
    r = lax.dynamic_update_slice(
        r, jnp.broadcast_to(meta.astype(x.dtype)[None], (batch, N_META, d)), (0, META0, 0))
    r = r.reshape(n_tok, d)

    c_f = 3 * att_w
    c_u = c_f + heads
    m_q, m_k, m_v = 0, att_w, 2 * att_w
    m_ua = 3 * att_w
    m_ub = m_ua + conv_c
    m_ga = m_ub + conv_c
    m_gc = m_ga + d

    valid_tok = jnp.tile(jnp.arange(lp) >= META0, batch)
    n_pairs = 2 * batch * (lp - META0)
    n_tiles_max = n_pairs // MOE_TM + N_EXPERTS
    n_rows = n_tiles_max * MOE_TM

    out = None
    for l in range(depth):
        wf = jnp.pad(w_in[l, :, c_f:c_u], ((0, 0), (0, LANES - heads))).astype(BF16)
        z, f = _inproj(r, norm_mix[l][None], w_in[l, :, :c_f].astype(BF16),
                       w_in[l, :, c_u:].astype(BF16), wf)
        cum = _forget_cumsum(f.reshape(batch, lp, LANES),
                             jnp.pad(b_forget[l], (0, LANES - heads))[None])
        z3 = z.reshape(batch, lp, -1)
        att = _attention(z3, cum, m_q, m_k, m_v).reshape(n_tok, att_w)
        conv_w_pad = jnp.concatenate([conv_w[l], jnp.zeros((1, conv_c), F32)], axis=0)
        cv = _conv_branch(z3, m_ua, m_ub, conv_w_pad, conv_b[l][None], conv_ln_g[l][None],
                          conv_ln_b[l][None]).reshape(n_tok, conv_c)

        wr = jnp.zeros((d, LANES), F32)
        wr = wr.at[:, :N_GROUPS].set(w_router_group[l])
        wr = wr.at[:, N_GROUPS:N_GROUPS + N_EXPERTS].set(w_router_expert[l]).astype(BF16)
        br = jnp.zeros((1, LANES), F32)
        br = br.at[0, :N_GROUPS].set(b_router_group[l])
        br = br.at[0, N_GROUPS:N_GROUPS + N_EXPERTS].set(b_router_expert[l])
        r, n2, route, route_t, counts = _merge_route(
            att, cv, z, m_ga, m_gc, r, w_attn_out[l].astype(BF16), w_conv_out[l].astype(BF16),
            w_out[l].astype(BF16), norm_ffn[l][None], wr, br, tiles_per_seq)

        dest, tile_e, first, nxt, n_act, zflag = _routing_tables(route_t, counts, n_tiles_max, valid_tok)
        xs = _dispatch(dest, zflag, n2, n_rows, tiles_per_seq)
        y = _experts(tile_e, first, nxt, n_act, xs, w_gate_up, w_down, l)
        if l + 1 < depth:
            r = _combine(dest, r, route, y, tiles_per_seq)
        else:
            out = _combine(dest, r, route, y, tiles_per_seq, final_gain=norm_final[None])
    return out.reshape(batch, seq, d)
```
